```python
import jax, jax.numpy as jnp
from jax import lax
import numpy as np


D_MODEL = 2048
BATCH = 4
SEQ = 2048
DEPTH = 4

GRID_W = 64
CTX_LEN = 256
HEAD_DIM = 128
N_HEADS = D_MODEL // 2 // HEAD_DIM
N_KV_HEADS = N_HEADS // 4
GROUP = N_HEADS // N_KV_HEADS
WINDOW = 128
BLOCK = 128
ROPE_THETA = 10000.0
ROPE_AXIS = HEAD_DIM // 2
POOL_GROUPS = 4
POOL_CH = D_MODEL // 4 // POOL_GROUPS
POOL_WINDOWS = (2, 4, 8, 16)
SGU_GROUPS = 4
SGU_CH = D_MODEL // 4 // SGU_GROUPS
CHUNK = 128
Q_W = N_HEADS * HEAD_DIM
KV_W = N_KV_HEADS * HEAD_DIM
POOL_W = POOL_GROUPS * POOL_CH
SGU_W = SGU_GROUPS * SGU_CH
MIX_W = Q_W + POOL_W + SGU_W
IN_W = Q_W + 2 * KV_W + POOL_W + 2 * SGU_W
D_FF = -(-8 * D_MODEL // (3 * 256)) * 256
EPS = 1e-6

kernel_name = "hybrid_parallel_group_dit_block"


def rms_norm(x, g):
    xf = x.astype(jnp.float32)
    y = xf * lax.rsqrt(jnp.mean(xf * xf, axis=-1, keepdims=True) + EPS)
    return (y * g.astype(jnp.float32)).astype(x.dtype)


def layer_norm(x, g):
    xf = x.astype(jnp.float32)
    mu = jnp.mean(xf, axis=-1, keepdims=True)
    var = jnp.mean(jnp.square(xf - mu), axis=-1, keepdims=True)
    return ((xf - mu) * lax.rsqrt(var + EPS) * g.astype(jnp.float32)).astype(x.dtype)


def modulate(h, shift, scale):
    return h * (1 + scale) + shift


def axial_rope_tables(rows):
    inv_freq = ROPE_THETA ** (-jnp.arange(0, ROPE_AXIS, 2, dtype=jnp.float32) / ROPE_AXIS)
    row = jnp.repeat(jnp.arange(rows, dtype=jnp.float32), GRID_W)
    col = jnp.tile(jnp.arange(GRID_W, dtype=jnp.float32), rows)
    ang = jnp.stack([row[:, None] * inv_freq, col[:, None] * inv_freq], axis=1)
    return jnp.cos(ang), jnp.sin(ang)


def apply_rope(x, cos, sin):
    B, S, H, _ = x.shape
    xr = x.reshape(B, S, H, 2, 2, ROPE_AXIS // 2).astype(jnp.float32)
    x1, x2 = xr[..., 0, :], xr[..., 1, :]
    c = cos[None, :, None]
    s = sin[None, :, None]
    out = jnp.stack([x1 * c - x2 * s, x2 * c + x1 * s], axis=-2)
    return out.reshape(B, S, H, HEAD_DIM).astype(x.dtype)


def split_in(p):
    B, S = p.shape[:2]
    q = p[..., :Q_W].reshape(B, S, N_HEADS, HEAD_DIM)
    k = p[..., Q_W:Q_W + KV_W].reshape(B, S, N_KV_HEADS, HEAD_DIM)
    v = p[..., Q_W + KV_W:Q_W + 2 * KV_W].reshape(B, S, N_KV_HEADS, HEAD_DIM)
    o = Q_W + 2 * KV_W
    pool_in = p[..., o:o + POOL_W]
    u = p[..., o + POOL_W:o + POOL_W + SGU_W]
    z = p[..., o + POOL_W + SGU_W:]
    return q, k, v, pool_in, u, z


def sink_column(sink, shape):
    s = sink.astype(jnp.float32).reshape(N_KV_HEADS, GROUP)
    return jnp.broadcast_to(s[:, :, None, None], shape[:-1] + (1,))


def window_attention(q, k, v, kc, vc, sink):
    B, S = q.shape[:2]
    nb = S // BLOCK
    scale = HEAD_DIM ** -0.5
    qb = q.reshape(B, nb, BLOCK, N_KV_HEADS, GROUP, HEAD_DIM)
    pad = ((0, 0), (BLOCK, BLOCK), (0, 0), (0, 0))
    kp = jnp.pad(k, pad).reshape(B, nb + 2, BLOCK, N_KV_HEADS, HEAD_DIM)
    vp = jnp.pad(v, pad).reshape(B, nb + 2, BLOCK, N_KV_HEADS, HEAD_DIM)
    kb = jnp.concatenate([kp[:, :-2], kp[:, 1:-1], kp[:, 2:]], axis=2)
    vb = jnp.concatenate([vp[:, :-2], vp[:, 1:-1], vp[:, 2:]], axis=2)
    s_loc = jnp.einsum('bnqhgd,bnkhd->bnhgqk', qb, kb).astype(jnp.float32) * scale
    s_ctx = jnp.einsum('bnqhgd,bchd->bnhgqc', qb, kc).astype(jnp.float32) * scale
    qi = jnp.arange(BLOCK)
    kj = jnp.arange(3 * BLOCK)
    rel = kj[None, :] - BLOCK - qi[:, None]
    key_pos = jnp.arange(nb)[:, None] * BLOCK - BLOCK + kj[None, :]
    mask = (jnp.abs(rel) <= WINDOW)[None] & ((key_pos >= 0) & (key_pos < S))[:, None, :]
    s_loc = jnp.where(mask[None, :, None, None], s_loc, -jnp.inf)
    sink_l = jnp.broadcast_to(sink.astype(jnp.float32).reshape(N_KV_HEADS, GROUP)[None, None, :, :, None, None],
                              s_loc.shape[:-1] + (1,))
    p = jax.nn.softmax(jnp.concatenate([s_loc, s_ctx, sink_l], axis=-1), axis=-1)
    nk = 3 * BLOCK
    nc = kc.shape[1]
    p_loc = p[..., :nk].astype(v.dtype)
    p_ctx = p[..., nk:nk + nc].astype(v.dtype)
    out = (jnp.einsum('bnhgqk,bnkhd->bnqhgd', p_loc, vb)
           + jnp.einsum('bnhgqc,bchd->bnqhgd', p_ctx, vc))
    return out.reshape(B, S, Q_W)


def context_attention(q, k, v, sink):
    B, C = q.shape[:2]
    qg = q.reshape(B, C, N_KV_HEADS, GROUP, HEAD_DIM)
    s = jnp.einsum('bqhgd,bkhd->bhgqk', qg, k).astype(jnp.float32) * HEAD_DIM ** -0.5
    sink_c = jnp.broadcast_to(sink.astype(jnp.float32).reshape(N_KV_HEADS, GROUP)[None, :, :, None, None],
                              s.shape[:-1] + (1,))
    p = jax.nn.softmax(jnp.concatenate([s, sink_c], axis=-1), axis=-1)[..., :C].astype(v.dtype)
    out = jnp.einsum('bhgqk,bkhd->bqhgd', p, v)
    return out.reshape(B, C, Q_W)


def pool_mixer(p, w, ch_scale):
    B, S = p.shape[:2]
    pg = p.reshape(B, S, POOL_GROUPS, POOL_CH)
    pf = pg.astype(jnp.float32)
    cs = jnp.concatenate([jnp.zeros((B, 1, POOL_GROUPS, POOL_CH), jnp.float32),
                          jnp.cumsum(pf, axis=1)], axis=1)
    half = jnp.array(POOL_WINDOWS, dtype=jnp.int32) // 2
    t = jnp.arange(S, dtype=jnp.int32)[:, None]
    lo = jnp.clip(t - half[None, :], 0, S)
    hi = jnp.clip(t + half[None, :], 0, S)
    gidx = jnp.arange(POOL_GROUPS)[None, :]
    win_sum = cs[:, hi, gidx, :] - cs[:, lo, gidx, :]
    cnt = (hi - lo).astype(jnp.float32)[None, :, :, None]
    pooled = (win_sum / cnt - pf).astype(p.dtype)
    out = jnp.einsum('bsgc,gcd->bsgd', pooled, w) * ch_scale.reshape(POOL_GROUPS, POOL_CH)
    return out.reshape(B, S, POOL_W)


def sgu_mixer(u, z, norm_g, w_s, b_s):
    B, S = u.shape[:2]
    n = S // CHUNK
    u = jax.nn.gelu(u, approximate=False).reshape(B, n, CHUNK, SGU_GROUPS, SGU_CH)
    z = jax.nn.gelu(z, approximate=False).reshape(B, S, SGU_GROUPS, SGU_CH)
    z = layer_norm(z, norm_g).reshape(B, n, CHUNK, SGU_GROUPS, SGU_CH)
    mixed = jnp.einsum('gpq,bnqgc->bnpgc', w_s, z) + b_s.T[None, None, :, :, None]
    return (u * mixed).reshape(B, S, SGU_W)


def swiglu(h, w_gu, w_dn):
    g, up = jnp.split(h @ w_gu, 2, axis=-1)
    return (jax.nn.silu(g) * up) @ w_dn


def setup_inputs(seed: int = 0) -> dict:
    key = jax.random.key(seed)
    ks = jax.random.split(key, 19)
    f = jnp.float32
    nrm = lambda k, shape, s: jax.random.normal(k, shape, f) * s
    return {
        "x": nrm(ks[0], (BATCH, SEQ, D_MODEL), 1.0),
        "c": nrm(ks[1], (BATCH, D_MODEL), 1.0),
        "ctx": nrm(ks[2], (BATCH, CTX_LEN, D_MODEL), 1.0),
        "c_ctx": nrm(ks[3], (D_MODEL,), 1.0),
        "w_ada": nrm(ks[4], (DEPTH, D_MODEL, 6 * D_MODEL), 0.5 * D_MODEL ** -0.5),
        "b_ada": nrm(ks[5], (DEPTH, 6 * D_MODEL), 0.01),
        "norm_mix_g": 1.0 + nrm(ks[6], (DEPTH, D_MODEL), 0.02),
        "norm_ffn_g": 1.0 + nrm(ks[7], (DEPTH, D_MODEL), 0.02),
        "w_in": nrm(ks[8], (DEPTH, D_MODEL, IN_W), D_MODEL ** -0.5),
        "attn_sink": nrm(ks[9], (DEPTH, N_HEADS), 0.5),
        "pool_w": nrm(ks[10], (DEPTH, POOL_GROUPS, POOL_CH, POOL_CH), POOL_CH ** -0.5),
        "pool_scale": 1.0 + nrm(ks[11], (DEPTH, POOL_W), 0.02),
        "sgu_norm_g": 1.0 + nrm(ks[12], (DEPTH, SGU_GROUPS, SGU_CH), 0.02),
        "sgu_w": nrm(ks[13], (DEPTH, SGU_GROUPS, CHUNK, CHUNK), CHUNK ** -0.5),
        "sgu_b": 1.0 + nrm(ks[14], (DEPTH, SGU_GROUPS, CHUNK), 0.02),
        "w_out": nrm(ks[15], (DEPTH, MIX_W, D_MODEL), MIX_W ** -0.5),
        "w_gate_up": nrm(ks[16], (DEPTH, D_MODEL, 2 * D_FF), D_MODEL ** -0.5),
        "w_down": nrm(ks[17], (DEPTH, D_FF, D_MODEL), D_FF ** -0.5),
        "final_norm_g": 1.0 + nrm(ks[18], (D_MODEL,), 0.02),
    }


def reference(x, c, ctx, c_ctx, w_ada, b_ada, norm_mix_g, norm_ffn_g, w_in, attn_sink, pool_w, pool_scale,
              sgu_norm_g, sgu_w, sgu_b, w_out, w_gate_up, w_down, final_norm_g):
    B = x.shape[0]
    C = ctx.shape[1]
    rows = x.shape[1] // GRID_W
    cos, sin = axial_rope_tables(rows)
    silu_c = jax.nn.silu(c)
    silu_cc = jax.nn.silu(c_ctx)
    h_ctx = ctx
    for l in range(DEPTH):
        last = l == DEPTH - 1
        mx = jnp.split((silu_c @ w_ada[l] + b_ada[l])[:, None, :], 6, axis=-1)
        mc = jnp.split(silu_cc @ w_ada[l] + b_ada[l], 6, axis=-1)
        hc = modulate(rms_norm(h_ctx, norm_mix_g[l]), mc[0], mc[1])
        if last:
            kc, vc = jnp.split(hc @ w_in[l][:, Q_W:Q_W + 2 * KV_W], 2, axis=-1)
            kc = kc.reshape(B, C, N_KV_HEADS, HEAD_DIM)
            vc = vc.reshape(B, C, N_KV_HEADS, HEAD_DIM)
        else:
            qc, kc, vc, poolc, uc, zc = split_in(hc @ w_in[l])
        hx = modulate(rms_norm(x, norm_mix_g[l]), mx[0], mx[1])
        qx, kx, vx, poolx, ux, zx = split_in(hx @ w_in[l])
        qx = apply_rope(qx, cos, sin)
        kx = apply_rope(kx, cos, sin)
        mix_x = jnp.concatenate([
            window_attention(qx, kx, vx, kc, vc, attn_sink[l]),
            pool_mixer(poolx, pool_w[l], pool_scale[l]),
            sgu_mixer(ux, zx, sgu_norm_g[l], sgu_w[l], sgu_b[l]),
        ], axis=-1)
        x = x + mx[2] * (mix_x @ w_out[l])
        x = x + mx[5] * swiglu(modulate(rms_norm(x, norm_ffn_g[l]), mx[3], mx[4]), w_gate_up[l], w_down[l])
        if not last:
            mix_c = jnp.concatenate([
                context_attention(qc, kc, vc, attn_sink[l]),
                pool_mixer(poolc, pool_w[l], pool_scale[l]),
                sgu_mixer(uc, zc, sgu_norm_g[l], sgu_w[l], sgu_b[l]),
            ], axis=-1)
            h_ctx = h_ctx + mc[2] * (mix_c @ w_out[l])
            h_ctx = h_ctx + mc[5] * swiglu(modulate(rms_norm(h_ctx, norm_ffn_g[l]), mc[3], mc[4]),
                                           w_gate_up[l], w_down[l])
    return rms_norm(x, final_norm_g)
```

```python
import functools

import jax
import jax.numpy as jnp
from jax import lax
from jax.experimental import pallas as pl
from jax.experimental.pallas import tpu as pltpu

F32 = jnp.float32
BF16 = jnp.bfloat16

GRID_W = 64
HEAD_DIM = 128
GROUP = 4
WINDOW_BLOCK = 128
ROPE_THETA = 10000.0
ROPE_AXIS = HEAD_DIM // 2
POOL_WINDOWS = (2, 4, 8, 16)
POOL_HALO = 8
EPS = 1e-6
NEG = -1e30

VMEM_LIMIT = 56 * 1024 * 1024
ROW_TILE = 512
FFN_ROW_TILE = 1024
FFN_COL_TILE = 512
NORM_CHUNK = 256


def _cparams(sem):
    return pltpu.CompilerParams(dimension_semantics=sem, vmem_limit_bytes=VMEM_LIMIT)


def _dot(a, b):
    return jnp.dot(a, b, preferred_element_type=F32)


def _ada_kernel(c_ref, w_ref, b_ref, o_ref):
    cv = c_ref[...]
    a = (cv * jax.nn.sigmoid(cv)).astype(BF16)
    o_ref[...] = _dot(a, w_ref[...].astype(BF16)) + b_ref[...]


def _ada_table(cond, w_ada, b_ada):
    depth, d, n = w_ada.shape
    tn = 1024
    return pl.pallas_call(
        _ada_kernel,
        grid=(depth, n // tn),
        in_specs=[
            pl.BlockSpec((8, d), lambda l, j: (0, 0)),
            pl.BlockSpec((None, d, tn), lambda l, j: (l, 0, j)),
            pl.BlockSpec((None, 1, tn), lambda l, j: (l, 0, j)),
        ],
        out_specs=pl.BlockSpec((None, 8, tn), lambda l, j: (l, 0, j)),
        out_shape=jax.ShapeDtypeStruct((depth, 8, n), F32),
        compiler_params=_cparams(("parallel", "parallel")),
        name="ada_table",
    )(cond, w_ada, b_ada.reshape(depth, 1, n))


def _modulated_norm(x, g, shift, scale):
    ms = jnp.mean(x * x, axis=-1, keepdims=True)
    h = x * lax.rsqrt(ms + EPS) * g
    return h * (1.0 + scale) + shift


def _inproj_kernel(x_ref, g_ref, sh_ref, sc_ref, cos_ref, sin_ref, w_ref,
                   q_ref, kv_ref, puz_ref, *, n_lat_tiles, q_w, kv_w):
    i = pl.program_id(0)
    tm = x_ref.shape[0]
    hb = _modulated_norm(x_ref[...], g_ref[...], sh_ref[...], sc_ref[...]).astype(BF16)
    pq = _dot(hb, w_ref[:, :q_w])
    pkv = _dot(hb, w_ref[:, q_w:q_w + 2 * kv_w])
    puz_ref[...] = _dot(hb, w_ref[:, q_w + 2 * kv_w:])

    @pl.when(i < n_lat_tiles)
    def _():
        cos = cos_ref[...]
        sin = sin_ref[...]
        lane = lax.broadcasted_iota(jnp.int32, (tm, HEAD_DIM), 1)
        first_half = (lane % ROPE_AXIS) < (ROPE_AXIS // 2)

        def rope(t):
            rot = jnp.where(first_half,
                            pltpu.roll(t, HEAD_DIM - ROPE_AXIS // 2, 1),
                            pltpu.roll(t, ROPE_AXIS // 2, 1))
            return t * cos + rot * sin

        for hd in range(q_w // HEAD_DIM):
            sl = slice(hd * HEAD_DIM, (hd + 1) * HEAD_DIM)
            q_ref[:, sl] = rope(pq[:, sl]).astype(BF16)
        for hd in range(kv_w // HEAD_DIM):
            sl = slice(hd * HEAD_DIM, (hd + 1) * HEAD_DIM)
            kv_ref[:, sl] = rope(pkv[:, sl]).astype(BF16)
        kv_ref[:, kv_w:] = pkv[:, kv_w:].astype(BF16)

    @pl.when(i >= n_lat_tiles)
    def _():
        q_ref[...] = pq.astype(BF16)
        kv_ref[...] = pkv.astype(BF16)


def _inproj(xt, norm_g, modr, cos, sin, w_in, l, *, n_lat_rows, seq):
    t, d = xt.shape
    in_w = w_in.shape[-1]
    q_w = d // 2
    kv_w = q_w // GROUP
    puz_w = in_w - q_w - 2 * kv_w
    tm = ROW_TILE
    tiles_per_seq = seq // tm
    n_lat_tiles = n_lat_rows // tm
    n_groups = n_lat_tiles // tiles_per_seq

    def mod_idx(k):
        return lambda i: (l * 48 + jnp.minimum(i // tiles_per_seq, n_groups) * 6 + k, 0, 0)

    kern = functools.partial(_inproj_kernel, n_lat_tiles=n_lat_tiles, q_w=q_w, kv_w=kv_w)
    return pl.pallas_call(
        kern,
        grid=(t // tm,),
        in_specs=[
            pl.BlockSpec((tm, d), lambda i: (i, 0)),
            pl.BlockSpec((None, 1, d), lambda i: (l, 0, 0)),
            pl.BlockSpec((None, 1, d), mod_idx(0)),
            pl.BlockSpec((None, 1, d), mod_idx(1)),
            pl.BlockSpec((tm, HEAD_DIM), lambda i: (i % tiles_per_seq, 0)),
            pl.BlockSpec((tm, HEAD_DIM), lambda i: (i % tiles_per_seq, 0)),
            pl.BlockSpec((None, d, in_w), lambda i: (l, 0, 0), pipeline_mode=pl.Buffered(1)),
        ],
        out_specs=[
            pl.BlockSpec((tm, q_w), lambda i: (i, 0)),
            pl.BlockSpec((tm, 2 * kv_w), lambda i: (i, 0)),
            pl.BlockSpec((tm, puz_w), lambda i: (i, 0)),
        ],
        out_shape=[
            jax.ShapeDtypeStruct((t, q_w), BF16),
            jax.ShapeDtypeStruct((t, 2 * kv_w), BF16),
            jax.ShapeDtypeStruct((t, puz_w), F32),
        ],
        compiler_params=_cparams(("parallel",)),
        name="in_proj",
    )(xt, norm_g, modr, modr, cos, sin, w_in)


def _gelu(x):
    return 0.5 * x * (1.0 + lax.erf(x * (0.5 ** 0.5)))


def _mixer_kernel(sink_ref, q_ref, kvp_ref, kvc_ref, kvn_ref, kvx_ref, puz_ref,
                  php_ref, phn_ref, pw_ref, ps_ref, sg_ref, sw_ref, sb_ref,
                  o_ref, ext_ref, *, n_lat_blocks, lat_blocks_per_seq, ctx_blocks_per_seq,
                  n_kv_heads, pool_w, sgu_w):
    blk = WINDOW_BLOCK
    hd = HEAD_DIM
    i = pl.program_id(0)
    is_lat = i < n_lat_blocks
    n = jnp.where(is_lat, i % lat_blocks_per_seq, (i - n_lat_blocks) % ctx_blocks_per_seq)
    nblk = jnp.where(is_lat, lat_blocks_per_seq, ctx_blocks_per_seq)
    has_prev = n > 0
    has_next = n < nblk - 1
    kv_w = n_kv_heads * hd
    q_w = kv_w * GROUP
    scale = hd ** -0.5

    def attend(keys, vals, valid, kvh):
        qs = jnp.concatenate(
            [q_ref[:, (kvh * GROUP + g) * hd:(kvh * GROUP + g + 1) * hd] for g in range(GROUP)],
            axis=0)
        s = lax.dot_general(qs, keys, (((1,), (1,)), ((), ())),
                            preferred_element_type=F32) * scale
        es, dens = [], []
        for g in range(GROUP):
            sg = s[g * blk:(g + 1) * blk]
            if valid is not None:
                sg = jnp.where(valid, sg, NEG)
            snk = sink_ref[kvh * GROUP + g]
            m = jnp.maximum(jnp.max(sg, axis=-1, keepdims=True), snk)
            e = jnp.exp(sg - m)
            dens.append(jnp.sum(e, axis=-1, keepdims=True) + jnp.exp(snk - m))
            es.append(e.astype(BF16))
        o = _dot(jnp.concatenate(es, axis=0), vals)
        for g in range(GROUP):
            h = kvh * GROUP + g
            o_ref[:, h * hd:(h + 1) * hd] = (o[g * blk:(g + 1) * blk] * (1.0 / dens[g])).astype(BF16)

    @pl.when(is_lat)
    def _():
        nk = 3 * blk + kvx_ref.shape[0]
        qi = lax.broadcasted_iota(jnp.int32, (blk, nk), 0)
        cj = lax.broadcasted_iota(jnp.int32, (blk, nk), 1)
        off_prev = jnp.where(has_prev, 0, 4 * blk)
        off_next = jnp.where(has_next, 0, 4 * blk)
        valid = ((cj >= qi + off_prev) | (cj >= blk)) & (
            (cj - 2 * blk <= qi - off_next) | (cj < 2 * blk) | (cj >= 3 * blk))
        for kvh in range(n_kv_heads):
            ks = slice(kvh * hd, (kvh + 1) * hd)
            vs = slice(kv_w + kvh * hd, kv_w + (kvh + 1) * hd)
            keys = jnp.concatenate([kvp_ref[:, ks], kvc_ref[:, ks], kvn_ref[:, ks], kvx_ref[:, ks]], axis=0)
            vals = jnp.concatenate([kvp_ref[:, vs], kvc_ref[:, vs], kvn_ref[:, vs], kvx_ref[:, vs]], axis=0)
            attend(keys, vals, valid, kvh)

    @pl.when(jnp.logical_not(is_lat))
    def _():
        for kvh in range(n_kv_heads):
            attend(kvx_ref[:, kvh * hd:(kvh + 1) * hd],
                   kvx_ref[:, kv_w + kvh * hd:kv_w + (kvh + 1) * hd], None, kvh)

    halo = POOL_HALO

    @pl.when(has_prev)
    def _():
        ext_ref[0:halo, :] = php_ref[...]

    @pl.when(jnp.logical_not(has_prev))
    def _():
        ext_ref[0:halo, :] = jnp.zeros((halo, pool_w), F32)

    @pl.when(has_next)
    def _():
        ext_ref[halo + blk:, :] = phn_ref[...]

    @pl.when(jnp.logical_not(has_next))
    def _():
        ext_ref[halo + blk:, :] = jnp.zeros((halo, pool_w), F32)

    ext_ref[halo:halo + blk, :] = puz_ref[:, :pool_w]
    pos = n * blk + lax.broadcasted_iota(jnp.int32, (blk, hd), 0)
    seq_len = nblk * blk
    for g, win in enumerate(POOL_WINDOWS):
        cs = slice(g * hd, (g + 1) * hd)
        half = win // 2
        acc = ext_ref[halo - half:halo - half + blk, cs]
        for d in range(-half + 1, half):
            acc = acc + ext_ref[halo + d:halo + d + blk, cs]
        cnt = (jnp.minimum(pos + half, seq_len) - jnp.maximum(pos - half, 0)).astype(F32)
        pooled = acc / cnt - puz_ref[:, cs]
        y = _dot(pooled.astype(BF16), pw_ref[g]) * ps_ref[:, cs]
        o_ref[:, q_w + g * hd:q_w + (g + 1) * hd] = y.astype(BF16)

    for g in range(sgu_w // hd):
        cs = slice(g * hd, (g + 1) * hd)
        u = _gelu(puz_ref[:, pool_w + g * hd:pool_w + (g + 1) * hd])
        z = _gelu(puz_ref[:, pool_w + sgu_w + g * hd:pool_w + sgu_w + (g + 1) * hd])
        mu = jnp.mean(z, axis=-1, keepdims=True)
        zc = z - mu
        var = jnp.mean(zc * zc, axis=-1, keepdims=True)
        zn = zc * lax.rsqrt(var + EPS) * sg_ref[:, cs]
        mixed = _dot(sw_ref[g], zn.astype(BF16)) + sb_ref[g]
        o_ref[:, q_w + pool_w + g * hd:q_w + pool_w + (g + 1) * hd] = (u * mixed).astype(BF16)


def _mixer(q, kv, puz, sink, pool_w_l, pool_scale_l, sgu_g_l, sgu_w_l, sgu_b_l, *,
           n_lat_rows, seq, ctx_len, with_ctx):
    t = q.shape[0]
    blk = WINDOW_BLOCK
    q_w = q.shape[1]
    kv_w = kv.shape[1] // 2
    pool_w = pool_scale_l.shape[-1]
    sgu_w = sgu_g_l.shape[-1]
    d_mix = q_w + pool_w + sgu_w
    n_lat = n_lat_rows // blk
    lat_per = seq // blk
    ctx_per = ctx_len // blk
    rows = t if with_ctx else n_lat_rows
    hpb = blk // POOL_HALO

    def seq_pos(i):
        lat = i < n_lat
        n = jnp.where(lat, i % lat_per, (i - n_lat) % ctx_per)
        nblk = jnp.where(lat, lat_per, ctx_per)
        return n, nblk

    def prev_idx(i):
        n, _ = seq_pos(i)
        return jnp.where(n > 0, i - 1, i)

    def next_idx(i):
        n, nblk = seq_pos(i)
        return jnp.where(n < nblk - 1, i + 1, i)

    def ctx_idx(i):
        b = jnp.where(i < n_lat, i // lat_per, (i - n_lat) // ctx_per)
        return n_lat_rows // ctx_len + b

    def halo_prev_idx(i):
        n, _ = seq_pos(i)
        return jnp.where(n > 0, i * hpb - 1, i * hpb)

    def halo_next_idx(i):
        n, nblk = seq_pos(i)
        return jnp.where(n < nblk - 1, (i + 1) * hpb, i * hpb)

    kern = functools.partial(
        _mixer_kernel, n_lat_blocks=n_lat, lat_blocks_per_seq=lat_per,
        ctx_blocks_per_seq=ctx_per, n_kv_heads=kv_w // HEAD_DIM, pool_w=pool_w, sgu_w=sgu_w)
    n_pool = pool_w // HEAD_DIM
    n_sgu = sgu_w // HEAD_DIM
    return pl.pallas_call(
        kern,
        grid=(rows // blk,),
        in_specs=[
            pl.BlockSpec(memory_space=pltpu.SMEM),
            pl.BlockSpec((blk, q_w), lambda i: (i, 0)),
            pl.BlockSpec((blk, 2 * kv_w), lambda i: (prev_idx(i), 0)),
            pl.BlockSpec((blk, 2 * kv_w), lambda i: (i, 0)),
            pl.BlockSpec((blk, 2 * kv_w), lambda i: (next_idx(i), 0)),
            pl.BlockSpec((ctx_len, 2 * kv_w), lambda i: (ctx_idx(i), 0)),
            pl.BlockSpec((blk, pool_w + 2 * sgu_w), lambda i: (i, 0)),
            pl.BlockSpec((POOL_HALO, pool_w), lambda i: (halo_prev_idx(i), 0)),
            pl.BlockSpec((POOL_HALO, pool_w), lambda i: (halo_next_idx(i), 0)),
            pl.BlockSpec((n_pool, HEAD_DIM, HEAD_DIM), lambda i: (0, 0, 0)),
            pl.BlockSpec((1, pool_w), lambda i: (0, 0)),
            pl.BlockSpec((1, sgu_w), lambda i: (0, 0)),
            pl.BlockSpec((n_sgu, blk, blk), lambda i: (0, 0, 0)),
            pl.BlockSpec((n_sgu, blk, HEAD_DIM), lambda i: (0, 0, 0)),
        ],
        out_specs=pl.BlockSpec((blk, d_mix), lambda i: (i, 0)),
        out_shape=jax.ShapeDtypeStruct((rows, d_mix), BF16),
        scratch_shapes=[pltpu.VMEM((blk + 2 * POOL_HALO, pool_w), F32)],
        compiler_params=_cparams(("parallel",)),
        name="mixer",
    )(sink, q, kv, kv, kv, kv, puz, puz, puz, pool_w_l, pool_scale_l, sgu_g_l, sgu_w_l, sgu_b_l)


def _outproj_kernel(mix_ref, x_ref, gate_ref, w_ref, o_ref):
    o_ref[...] = x_ref[...] + gate_ref[...] * _dot(mix_ref[...], w_ref[...])


def _outproj(mix, xt, modr, w_out, l, *, seq, n_lat_rows):
    rows, d_mix = mix.shape
    d = xt.shape[1]
    tm = ROW_TILE
    tiles_per_seq = seq // tm
    n_groups = n_lat_rows // seq
    return pl.pallas_call(
        _outproj_kernel,
        grid=(rows // tm,),
        in_specs=[
            pl.BlockSpec((tm, d_mix), lambda i: (i, 0)),
            pl.BlockSpec((tm, d), lambda i: (i, 0)),
            pl.BlockSpec((None, 1, d),
                         lambda i: (l * 48 + jnp.minimum(i // tiles_per_seq, n_groups) * 6 + 2, 0, 0)),
            pl.BlockSpec((None, d_mix, d), lambda i: (l, 0, 0), pipeline_mode=pl.Buffered(1)),
        ],
        out_specs=pl.BlockSpec((tm, d), lambda i: (i, 0)),
        out_shape=jax.ShapeDtypeStruct((rows, d), F32),
        compiler_params=_cparams(("parallel",)),
        name="out_proj",
    )(mix, xt, modr, w_out)


def _ffn_kernel(x_ref, g_ref, sh_ref, sc_ref, gate_ref, wg_ref, wu_ref, wd_ref, fg_ref,
                o_ref, xn_ref, *, final_norm):
    f = pl.program_id(1)
    nf = pl.num_programs(1)
    tm, d = x_ref.shape
    n_chunks = tm // NORM_CHUNK

    @pl.when(f == 0)
    def _():
        def body(r, carry):
            rows = pl.ds(pl.multiple_of(r * NORM_CHUNK, NORM_CHUNK), NORM_CHUNK)
            h = _modulated_norm(x_ref[rows, :], g_ref[...], sh_ref[...], sc_ref[...])
            xn_ref[rows, :] = h.astype(BF16)
            o_ref[rows, :] = jnp.zeros((NORM_CHUNK, d), F32)
            return carry
        lax.fori_loop(0, n_chunks, body, 0)

    xn = xn_ref[...]
    gg = _dot(xn, wg_ref[...])
    uu = _dot(xn, wu_ref[...])
    act = (gg * jax.nn.sigmoid(gg) * uu).astype(BF16)
    for cidx in range(d // FFN_COL_TILE):
        cs = slice(cidx * FFN_COL_TILE, (cidx + 1) * FFN_COL_TILE)
        o_ref[:, cs] += _dot(act, wd_ref[:, cs])

    @pl.when(f == nf - 1)
    def _():
        def body(r, carry):
            rows = pl.ds(pl.multiple_of(r * NORM_CHUNK, NORM_CHUNK), NORM_CHUNK)
            y = x_ref[rows, :] + gate_ref[...] * o_ref[rows, :]
            if final_norm:
                ms = jnp.mean(y * y, axis=-1, keepdims=True)
                y = y * lax.rsqrt(ms + EPS) * fg_ref[...]
            o_ref[rows, :] = y
            return carry
        lax.fori_loop(0, n_chunks, body, 0)


def _ffn(xt, norm_g, modr, w_gu, w_dn, final_g, l, *, rows, seq, n_lat_rows, final_norm):
    d = xt.shape[1]
    d_ff = w_dn.shape[1]
    tm = FFN_ROW_TILE
    tf = FFN_COL_TILE
    nf = d_ff // tf
    tiles_per_seq = seq // tm
    n_groups = n_lat_rows // seq

    def mod_idx(k):
        return lambda i, f: (l * 48 + jnp.minimum(i // tiles_per_seq, n_groups) * 6 + k, 0, 0)

    kern = functools.partial(_ffn_kernel, final_norm=final_norm)
    return pl.pallas_call(
        kern,
        grid=(rows // tm, nf),
        in_specs=[
            pl.BlockSpec((tm, d), lambda i, f: (i, 0), pipeline_mode=pl.Buffered(1)),
            pl.BlockSpec((None, 1, d), lambda i, f: (l, 0, 0)),
            pl.BlockSpec((None, 1, d), mod_idx(3)),
            pl.BlockSpec((None, 1, d), mod_idx(4)),
            pl.BlockSpec((None, 1, d), mod_idx(5)),
            pl.BlockSpec((None, d, tf), lambda i, f: (l, 0, f)),
            pl.BlockSpec((None, d, tf), lambda i, f: (l, 0, nf + f)),
            pl.BlockSpec((None, tf, d), lambda i, f: (l, f, 0)),
            pl.BlockSpec((1, d), lambda i, f: (0, 0)),
        ],
        out_specs=pl.BlockSpec((tm, d), lambda i, f: (i, 0)),
        out_shape=jax.ShapeDtypeStruct((rows, d), F32),
        scratch_shapes=[pltpu.VMEM((tm, d), BF16)],
        compiler_params=_cparams(("parallel", "arbitrary")),
        name="ffn",
    )(xt, norm_g, modr, modr, modr, w_gu, w_gu, w_dn, final_g)


def _rope_tables(seq):
    rows = seq // GRID_W
    inv_freq = ROPE_THETA ** (-jnp.arange(0, ROPE_AXIS, 2, dtype=F32) / ROPE_AXIS)
    row = jnp.repeat(jnp.arange(rows, dtype=F32), GRID_W)
    col = jnp.tile(jnp.arange(GRID_W, dtype=F32), rows)
    ar = row[:, None] * inv_freq
    ac = col[:, None] * inv_freq
    cos = jnp.concatenate([jnp.cos(ar), jnp.cos(ar), jnp.cos(ac), jnp.cos(ac)], axis=1)
    sin = jnp.concatenate([-jnp.sin(ar), jnp.sin(ar), -jnp.sin(ac), jnp.sin(ac)], axis=1)
    return cos, sin


def kernel(x, c, ctx, c_ctx, w_ada, b_ada, norm_mix_g, norm_ffn_g, w_in, attn_sink, pool_w, pool_scale,
           sgu_norm_g, sgu_w, sgu_b, w_out, w_gate_up, w_down, final_norm_g):
    bsz, seq, d = x.shape
    ctx_len = ctx.shape[1]
    depth = w_in.shape[0]
    n_lat_rows = bsz * seq

    cond = jnp.concatenate([c, c_ctx[None, :], jnp.zeros((8 - bsz - 1, d), F32)], axis=0)
    mod = _ada_table(cond, w_ada, b_ada)
    modr = mod.reshape(depth * 8 * 6, 1, d)

    cos, sin = _rope_tables(seq)
    w_in_b = w_in.astype(BF16)
    w_out_b = w_out.astype(BF16)
    w_gu_b = w_gate_up.astype(BF16)
    w_dn_b = w_down.astype(BF16)
    pool_w_b = pool_w.astype(BF16)
    sgu_w_b = sgu_w.astype(BF16)
    norm_mix = norm_mix_g.reshape(depth, 1, d)
    norm_ffn = norm_ffn_g.reshape(depth, 1, d)
    final_g = final_norm_g.reshape(1, d)

    xt = jnp.concatenate([x.reshape(n_lat_rows, d), ctx.reshape(bsz * ctx_len, d)], axis=0)

    for l in range(depth):
        last = l == depth - 1
        rows = n_lat_rows if last else xt.shape[0]
        q, kv, puz = _inproj(xt, norm_mix, modr, cos, sin, w_in_b, l, n_lat_rows=n_lat_rows, seq=seq)
        sgu_b_full = jnp.broadcast_to(sgu_b[l][:, :, None], sgu_b.shape[1:] + (HEAD_DIM,))
        mix = _mixer(q, kv, puz, attn_sink[l], pool_w_b[l], pool_scale[l].reshape(1, -1),
                     sgu_norm_g[l].reshape(1, -1), sgu_w_b[l], sgu_b_full,
                     n_lat_rows=n_lat_rows, seq=seq, ctx_len=ctx_len, with_ctx=not last)
        xt = _outproj(mix, xt, modr, w_out_b, l, seq=seq, n_lat_rows=n_lat_rows)
        xt = _ffn(xt, norm_ffn, modr, w_gu_b, w_dn_b, final_g, l, rows=rows, seq=seq,
                  n_lat_rows=n_lat_rows, final_norm=last)
    return xt.reshape(bsz, seq, d)
```

```python
import functools

import jax
import jax.numpy as jnp
from jax import lax
from jax.experimental import pallas as pl
from jax.experimental.pallas import tpu as pltpu

F32 = jnp.float32
BF16 = jnp.bfloat16

GRID_W = 64
HEAD_DIM = 128
GROUP = 4
WINDOW_BLOCK = 128
ROPE_THETA = 10000.0
ROPE_AXIS = HEAD_DIM // 2
POOL_WINDOWS = (2, 4, 8, 16)
POOL_HALO = 8
EPS = 1e-6
NEG = -1e30
LOG2E = 1.4426950408889634

MOD_ROWS = 8
MOD_CHUNKS = 6

VMEM_LIMIT = 56 * 1024 * 1024
ROW_TILE = 512
FFN_ROW_TILE = 1024
FFN_COL_TILE = 512
NORM_CHUNK = 32
NORM_UNROLL = 4


def _cparams(sem):
    return pltpu.CompilerParams(dimension_semantics=sem, vmem_limit_bytes=VMEM_LIMIT)


def _dot(a, b):
    return jnp.dot(a, b, preferred_element_type=F32)


def _mod_index(l, group, chunk):
    return (l * MOD_ROWS + group) * MOD_CHUNKS + chunk


def _ada_kernel(c_ref, w_ref, b_ref, o_ref):
    cv = c_ref[...]
    a = (cv * jax.nn.sigmoid(cv)).astype(BF16)
    o_ref[...] = _dot(a, w_ref[...].astype(BF16)) + b_ref[...]


def _ada_table(cond, w_ada, b_ada):
    depth, d, n = w_ada.shape
    tn = 1024
    return pl.pallas_call(
        _ada_kernel,
        grid=(depth, n // tn),
        in_specs=[
            pl.BlockSpec((MOD_ROWS, d), lambda l, j: (0, 0)),
            pl.BlockSpec((None, d, tn), lambda l, j: (l, 0, j)),
            pl.BlockSpec((None, 1, tn), lambda l, j: (l, 0, j)),
        ],
        out_specs=pl.BlockSpec((None, MOD_ROWS, tn), lambda l, j: (l, 0, j)),
        out_shape=jax.ShapeDtypeStruct((depth, MOD_ROWS, n), F32),
        compiler_params=_cparams(("parallel", "parallel")),
        name="ada_table",
    )(cond, w_ada, b_ada.reshape(depth, 1, n))


def _modulated_norm(x, gain, shift):
    ms = jnp.mean(x * x, axis=-1, keepdims=True)
    return x * lax.rsqrt(ms + EPS) * gain + shift


def _pick_rows(lat_ref, ctx_ref, rows, is_lat_tile):
    if ctx_ref is None:
        return lat_ref[rows, :]
    a = lat_ref[rows, :]
    sel = jnp.full(a.shape, is_lat_tile, jnp.int32) > 0
    return jnp.where(sel, a, ctx_ref[rows, :])


def _inproj_kernel(*refs, n_lat_tiles, q_w, kv_w, split_src):
    if split_src:
        xa_ref, xb_ref = refs[:2]
        refs = refs[2:]
    else:
        xa_ref, xb_ref = refs[0], None
        refs = refs[1:]
    g_ref, sh_ref, sc_ref, cos_ref, sin_ref, w_ref, q_ref, kv_ref, puz_ref, hb_ref = refs
    i = pl.program_id(0)
    tm = hb_ref.shape[0]
    is_lat = (i < n_lat_tiles).astype(jnp.int32)

    gain = g_ref[...] * (1.0 + sc_ref[...])
    shift = sh_ref[...]

    def body(r, carry):
        rows = pl.ds(pl.multiple_of(r * NORM_CHUNK, NORM_CHUNK), NORM_CHUNK)
        xs = _pick_rows(xa_ref, xb_ref, rows, is_lat)
        hb_ref[rows, :] = _modulated_norm(xs, gain, shift).astype(BF16)
        return carry
    lax.fori_loop(0, tm // NORM_CHUNK, body, 0, unroll=NORM_UNROLL)

    hb = hb_ref[...]
    pq = _dot(hb, w_ref[:, :q_w])
    pkv = _dot(hb, w_ref[:, q_w:q_w + 2 * kv_w])
    puz_ref[...] = _dot(hb, w_ref[:, q_w + 2 * kv_w:])

    cos = cos_ref[...]
    sin = sin_ref[...]
    lane = lax.broadcasted_iota(jnp.int32, (tm, HEAD_DIM), 1)
    first_half = (lane % ROPE_AXIS) < (ROPE_AXIS // 2)

    def rope(t):
        rot = jnp.where(first_half,
                        pltpu.roll(t, HEAD_DIM - ROPE_AXIS // 2, 1),
                        pltpu.roll(t, ROPE_AXIS // 2, 1))
        return t * cos + rot * sin

    for hd in range(q_w // HEAD_DIM):
        sl = slice(hd * HEAD_DIM, (hd + 1) * HEAD_DIM)
        q_ref[:, sl] = rope(pq[:, sl]).astype(BF16)
    for hd in range(kv_w // HEAD_DIM):
        sl = slice(hd * HEAD_DIM, (hd + 1) * HEAD_DIM)
        kv_ref[:, sl] = rope(pkv[:, sl]).astype(BF16)
    kv_ref[:, kv_w:] = pkv[:, kv_w:].astype(BF16)


def _row_sources(x_lat, x_ctx, tm, n_lat_tiles):
    d = x_lat.shape[1]
    lat_map = lambda i: (jnp.minimum(i, n_lat_tiles - 1), 0)
    ctx_map = lambda i: (jnp.maximum(i - n_lat_tiles, 0), 0)
    return [pl.BlockSpec((tm, d), lat_map), pl.BlockSpec((tm, d), ctx_map)], [x_lat, x_ctx]


def _inproj(xs, norm_g, modr, cos, sin, w_in, l, *, n_rows, n_lat_rows, seq):
    d = xs[0].shape[1]
    in_w = w_in.shape[-1]
    q_w = d // 2
    kv_w = q_w // GROUP
    puz_w = in_w - q_w - 2 * kv_w
    tm = ROW_TILE
    tiles_per_seq = seq // tm
    n_lat_tiles = n_lat_rows // tm
    n_groups = n_lat_tiles // tiles_per_seq
    split_src = len(xs) == 2

    def mod_idx(k):
        return lambda i: (_mod_index(l, jnp.minimum(i // tiles_per_seq, n_groups), k), 0, 0)

    def rope_idx(i):
        return (jnp.where(i < n_lat_tiles, i % tiles_per_seq, tiles_per_seq), 0)

    if split_src:
        x_specs, x_args = _row_sources(xs[0], xs[1], tm, n_lat_tiles)
    else:
        x_specs, x_args = [pl.BlockSpec((tm, d), lambda i: (i, 0))], [xs[0]]

    kern = functools.partial(_inproj_kernel, n_lat_tiles=n_lat_tiles, q_w=q_w, kv_w=kv_w,
                             split_src=split_src)
    return pl.pallas_call(
        kern,
        grid=(n_rows // tm,),
        in_specs=x_specs + [
            pl.BlockSpec((None, 1, d), lambda i: (l, 0, 0)),
            pl.BlockSpec((None, 1, d), mod_idx(0)),
            pl.BlockSpec((None, 1, d), mod_idx(1)),
            pl.BlockSpec((tm, HEAD_DIM), rope_idx),
            pl.BlockSpec((tm, HEAD_DIM), rope_idx),
            pl.BlockSpec((None, d, in_w), lambda i: (l, 0, 0), pipeline_mode=pl.Buffered(1)),
        ],
        out_specs=[
            pl.BlockSpec((tm, q_w), lambda i: (i, 0)),
            pl.BlockSpec((tm, 2 * kv_w), lambda i: (i, 0)),
            pl.BlockSpec((tm, puz_w), lambda i: (i, 0)),
        ],
        out_shape=[
            jax.ShapeDtypeStruct((n_rows, q_w), BF16),
            jax.ShapeDtypeStruct((n_rows, 2 * kv_w), BF16),
            jax.ShapeDtypeStruct((n_rows, puz_w), F32),
        ],
        scratch_shapes=[pltpu.VMEM((tm, d), BF16)],
        compiler_params=_cparams(("parallel",)),
        name="in_proj",
    )(*x_args, norm_g, modr, modr, cos, sin, w_in)


def _gelu(x):
    return 0.5 * x * (1.0 + lax.erf(x * (0.5 ** 0.5)))


def _mixer_kernel(sink_ref, q_ref, kvp_ref, kvc_ref, kvn_ref, kvx_ref, puz_ref,
                  php_ref, phn_ref, pw_ref, ps_ref, sg_ref, sw_ref, sb_ref,
                  o_ref, ext_ref, *, n_lat_blocks, lat_blocks_per_seq, ctx_blocks_per_seq,
                  n_kv_heads, pool_w, sgu_w):
    blk = WINDOW_BLOCK
    hd = HEAD_DIM
    i = pl.program_id(0)
    is_lat = i < n_lat_blocks
    n = jnp.where(is_lat, i % lat_blocks_per_seq, (i - n_lat_blocks) % ctx_blocks_per_seq)
    nblk = jnp.where(is_lat, lat_blocks_per_seq, ctx_blocks_per_seq)
    has_prev = n > 0
    has_next = n < nblk - 1
    kv_w = n_kv_heads * hd
    q_w = kv_w * GROUP
    logit_scale = hd ** -0.5 * LOG2E

    def attend(keys, vals, tile_masks, kvh):
        nk = keys.shape[0]
        qs = jnp.concatenate(
            [q_ref[:, (kvh * GROUP + g) * hd:(kvh * GROUP + g + 1) * hd] for g in range(GROUP)],
            axis=0)
        s = lax.dot_general(qs, keys, (((1,), (1,)), ((), ())),
                            preferred_element_type=F32) * logit_scale
        vals1 = jnp.concatenate([vals, jnp.ones((nk, hd), BF16)], axis=1)
        es, sink_terms = [], []
        for g in range(GROUP):
            tiles = []
            for t, mask in enumerate(tile_masks):
                st = s[g * blk:(g + 1) * blk, t * blk:(t + 1) * blk]
                tiles.append(st if mask is None else jnp.where(mask, st, NEG))
            mt = tiles[0]
            for st in tiles[1:]:
                mt = jnp.maximum(mt, st)
            snk = sink_ref[kvh * GROUP + g] * LOG2E
            m = jnp.maximum(jnp.max(mt, axis=-1, keepdims=True), snk)
            es.append(jnp.concatenate([jnp.exp2(st - m).astype(BF16) for st in tiles], axis=1))
            sink_terms.append(jnp.exp2(snk - m))
        o = _dot(jnp.concatenate(es, axis=0), vals1)
        for g in range(GROUP):
            h = kvh * GROUP + g
            og = o[g * blk:(g + 1) * blk]
            o_ref[:, h * hd:(h + 1) * hd] = (og[:, :hd] / (og[:, hd:] + sink_terms[g])).astype(BF16)

    @pl.when(is_lat)
    def _():
        qi = lax.broadcasted_iota(jnp.int32, (blk, blk), 0)
        kj = lax.broadcasted_iota(jnp.int32, (blk, blk), 1)
        mask_prev = kj >= qi + jnp.where(has_prev, 0, blk)
        mask_next = kj <= qi - jnp.where(has_next, 0, blk)
        masks = [mask_prev, None, mask_next] + [None] * (kvx_ref.shape[0] // blk)
        for kvh in range(n_kv_heads):
            ks = slice(kvh * hd, (kvh + 1) * hd)
            vs = slice(kv_w + kvh * hd, kv_w + (kvh + 1) * hd)
            keys = jnp.concatenate([kvp_ref[:, ks], kvc_ref[:, ks], kvn_ref[:, ks], kvx_ref[:, ks]], axis=0)
            vals = jnp.concatenate([kvp_ref[:, vs], kvc_ref[:, vs], kvn_ref[:, vs], kvx_ref[:, vs]], axis=0)
            attend(keys, vals, masks, kvh)

    @pl.when(jnp.logical_not(is_lat))
    def _():
        masks = [None] * (kvx_ref.shape[0] // blk)
        for kvh in range(n_kv_heads):
            attend(kvx_ref[:, kvh * hd:(kvh + 1) * hd],
                   kvx_ref[:, kv_w + kvh * hd:kv_w + (kvh + 1) * hd], masks, kvh)

    halo = POOL_HALO

    @pl.when(has_prev)
    def _():
        ext_ref[0:halo, :] = php_ref[...]

    @pl.when(jnp.logical_not(has_prev))
    def _():
        ext_ref[0:halo, :] = jnp.zeros((halo, pool_w), F32)

    @pl.when(has_next)
    def _():
        ext_ref[halo + blk:, :] = phn_ref[...]

    @pl.when(jnp.logical_not(has_next))
    def _():
        ext_ref[halo + blk:, :] = jnp.zeros((halo, pool_w), F32)

    ext_ref[halo:halo + blk, :] = puz_ref[:, :pool_w]
    pos = n * blk + lax.broadcasted_iota(jnp.int32, (blk, hd), 0)
    seq_len = nblk * blk
    for g, win in enumerate(POOL_WINDOWS):
        cs = slice(g * hd, (g + 1) * hd)
        half = win // 2
        acc = ext_ref[halo - half:halo - half + blk, cs]
        for d in range(-half + 1, half):
            acc = acc + ext_ref[halo + d:halo + d + blk, cs]
        cnt = (jnp.minimum(pos + half, seq_len) - jnp.maximum(pos - half, 0)).astype(F32)
        pooled = acc / cnt - puz_ref[:, cs]
        y = _dot(pooled.astype(BF16), pw_ref[g]) * ps_ref[:, cs]
        o_ref[:, q_w + g * hd:q_w + (g + 1) * hd] = y.astype(BF16)

    for g in range(sgu_w // hd):
        cs = slice(g * hd, (g + 1) * hd)
        u = _gelu(puz_ref[:, pool_w + g * hd:pool_w + (g + 1) * hd])
        z = _gelu(puz_ref[:, pool_w + sgu_w + g * hd:pool_w + sgu_w + (g + 1) * hd])
        mu = jnp.mean(z, axis=-1, keepdims=True)
        zc = z - mu
        var = jnp.mean(zc * zc, axis=-1, keepdims=True)
        zn = zc * lax.rsqrt(var + EPS) * sg_ref[:, cs]
        mixed = _dot(sw_ref[g], zn.astype(BF16)) + sb_ref[g]
        o_ref[:, q_w + pool_w + g * hd:q_w + pool_w + (g + 1) * hd] = (u * mixed).astype(BF16)


def _mixer(q, kv, puz, sink, pool_w_l, pool_scale_l, sgu_g_l, sgu_w_l, sgu_b_l, *,
           n_lat_rows, seq, ctx_len, with_ctx):
    t = q.shape[0]
    blk = WINDOW_BLOCK
    q_w = q.shape[1]
    kv_w = kv.shape[1] // 2
    pool_w = pool_scale_l.shape[-1]
    sgu_w = sgu_g_l.shape[-1]
    d_mix = q_w + pool_w + sgu_w
    n_lat = n_lat_rows // blk
    lat_per = seq // blk
    ctx_per = ctx_len // blk
    rows = t if with_ctx else n_lat_rows
    hpb = blk // POOL_HALO

    def seq_pos(i):
        lat = i < n_lat
        n = jnp.where(lat, i % lat_per, (i - n_lat) % ctx_per)
        nblk = jnp.where(lat, lat_per, ctx_per)
        return n, nblk

    def prev_idx(i):
        n, _ = seq_pos(i)
        return jnp.where(n > 0, i - 1, i)

    def next_idx(i):
        n, nblk = seq_pos(i)
        return jnp.where(n < nblk - 1, i + 1, i)

    def ctx_idx(i):
        b = jnp.where(i < n_lat, i // lat_per, (i - n_lat) // ctx_per)
        return n_lat_rows // ctx_len + b

    def halo_prev_idx(i):
        n, _ = seq_pos(i)
        return jnp.where(n > 0, i * hpb - 1, i * hpb)

    def halo_next_idx(i):
        n, nblk = seq_pos(i)
        return jnp.where(n < nblk - 1, (i + 1) * hpb, i * hpb)

    kern = functools.partial(
        _mixer_kernel, n_lat_blocks=n_lat, lat_blocks_per_seq=lat_per,
        ctx_blocks_per_seq=ctx_per, n_kv_heads=kv_w // HEAD_DIM, pool_w=pool_w, sgu_w=sgu_w)
    n_pool = pool_w // HEAD_DIM
    n_sgu = sgu_w // HEAD_DIM
    return pl.pallas_call(
        kern,
        grid=(rows // blk,),
        in_specs=[
            pl.BlockSpec(memory_space=pltpu.SMEM),
            pl.BlockSpec((blk, q_w), lambda i: (i, 0)),
            pl.BlockSpec((blk, 2 * kv_w), lambda i: (prev_idx(i), 0)),
            pl.BlockSpec((blk, 2 * kv_w), lambda i: (i, 0)),
            pl.BlockSpec((blk, 2 * kv_w), lambda i: (next_idx(i), 0)),
            pl.BlockSpec((ctx_len, 2 * kv_w), lambda i: (ctx_idx(i), 0)),
            pl.BlockSpec((blk, pool_w + 2 * sgu_w), lambda i: (i, 0)),
            pl.BlockSpec((POOL_HALO, pool_w), lambda i: (halo_prev_idx(i), 0)),
            pl.BlockSpec((POOL_HALO, pool_w), lambda i: (halo_next_idx(i), 0)),
            pl.BlockSpec((n_pool, HEAD_DIM, HEAD_DIM), lambda i: (0, 0, 0)),
            pl.BlockSpec((1, pool_w), lambda i: (0, 0)),
            pl.BlockSpec((1, sgu_w), lambda i: (0, 0)),
            pl.BlockSpec((n_sgu, blk, blk), lambda i: (0, 0, 0)),
            pl.BlockSpec((n_sgu, blk, HEAD_DIM), lambda i: (0, 0, 0)),
        ],
        out_specs=pl.BlockSpec((blk, d_mix), lambda i: (i, 0)),
        out_shape=jax.ShapeDtypeStruct((rows, d_mix), BF16),
        scratch_shapes=[pltpu.VMEM((blk + 2 * POOL_HALO, pool_w), F32)],
        compiler_params=_cparams(("parallel",)),
        name="mixer",
    )(sink, q, kv, kv, kv, kv, puz, puz, puz, pool_w_l, pool_scale_l, sgu_g_l, sgu_w_l, sgu_b_l)


def _outproj_kernel(*refs, n_lat_tiles, split_src):
    if split_src:
        mix_ref, xa_ref, xb_ref, gate_ref, w_ref, o_ref = refs
    else:
        mix_ref, xa_ref, gate_ref, w_ref, o_ref = refs
        xb_ref = None
    is_lat = (pl.program_id(0) < n_lat_tiles).astype(jnp.int32)
    x = _pick_rows(xa_ref, xb_ref, slice(None), is_lat)
    o_ref[...] = x + gate_ref[...] * _dot(mix_ref[...], w_ref[...])


def _outproj(mix, xs, modr, w_out, l, *, seq, n_lat_rows):
    rows, d_mix = mix.shape
    d = xs[0].shape[1]
    tm = ROW_TILE
    tiles_per_seq = seq // tm
    n_lat_tiles = n_lat_rows // tm
    n_groups = n_lat_rows // seq
    split_src = len(xs) == 2
    if split_src:
        x_specs, x_args = _row_sources(xs[0], xs[1], tm, n_lat_tiles)
    else:
        x_specs, x_args = [pl.BlockSpec((tm, d), lambda i: (i, 0))], [xs[0]]
    kern = functools.partial(_outproj_kernel, n_lat_tiles=n_lat_tiles, split_src=split_src)
    return pl.pallas_call(
        kern,
        grid=(rows // tm,),
        in_specs=[pl.BlockSpec((tm, d_mix), lambda i: (i, 0))] + x_specs + [
            pl.BlockSpec((None, 1, d),
                         lambda i: (_mod_index(l, jnp.minimum(i // tiles_per_seq, n_groups), 2), 0, 0)),
            pl.BlockSpec((None, d_mix, d), lambda i: (l, 0, 0), pipeline_mode=pl.Buffered(1)),
        ],
        out_specs=pl.BlockSpec((tm, d), lambda i: (i, 0)),
        out_shape=jax.ShapeDtypeStruct((rows, d), F32),
        compiler_params=_cparams(("parallel",)),
        name="out_proj",
    )(mix, *x_args, modr, w_out)


def _ffn_kernel(x_ref, g_ref, sh_ref, sc_ref, gate_ref, wg_ref, wu_ref, wd_ref, fg_ref,
                o_ref, xn_ref, *, final_norm):
    f = pl.program_id(1)
    nf = pl.num_programs(1)
    tm, d = x_ref.shape
    n_chunks = tm // NORM_CHUNK

    @pl.when(f == 0)
    def _():
        gain = g_ref[...] * (1.0 + sc_ref[...])
        shift = sh_ref[...]

        def body(r, carry):
            rows = pl.ds(pl.multiple_of(r * NORM_CHUNK, NORM_CHUNK), NORM_CHUNK)
            h = _modulated_norm(x_ref[rows, :], gain, shift)
            xn_ref[rows, :] = h.astype(BF16)
            o_ref[rows, :] = jnp.zeros((NORM_CHUNK, d), F32)
            return carry
        lax.fori_loop(0, n_chunks, body, 0, unroll=NORM_UNROLL)

    xn = xn_ref[...]
    gg = _dot(xn, wg_ref[...])
    uu = _dot(xn, wu_ref[...])
    act = (gg * jax.nn.sigmoid(gg) * uu).astype(BF16)
    for cidx in range(d // FFN_COL_TILE):
        cs = slice(cidx * FFN_COL_TILE, (cidx + 1) * FFN_COL_TILE)
        o_ref[:, cs] += _dot(act, wd_ref[:, cs])

    @pl.when(f == nf - 1)
    def _():
        def body(r, carry):
            rows = pl.ds(pl.multiple_of(r * NORM_CHUNK, NORM_CHUNK), NORM_CHUNK)
            y = x_ref[rows, :] + gate_ref[...] * o_ref[rows, :]
            if final_norm:
                ms = jnp.mean(y * y, axis=-1, keepdims=True)
                y = y * lax.rsqrt(ms + EPS) * fg_ref[...]
            o_ref[rows, :] = y
            return carry
        lax.fori_loop(0, n_chunks, body, 0, unroll=NORM_UNROLL)


def _ffn(xt, norm_g, modr, w_gu, w_dn, final_g, l, *, rows, seq, n_lat_rows, final_norm):
    d = xt.shape[1]
    d_ff = w_dn.shape[1]
    tm = FFN_ROW_TILE
    tf = FFN_COL_TILE
    nf = d_ff // tf
    tiles_per_seq = seq // tm
    n_groups = n_lat_rows // seq

    def mod_idx(k):
        return lambda i, f: (_mod_index(l, jnp.minimum(i // tiles_per_seq, n_groups), k), 0, 0)

    kern = functools.partial(_ffn_kernel, final_norm=final_norm)
    return pl.pallas_call(
        kern,
        grid=(rows // tm, nf),
        in_specs=[
            pl.BlockSpec((tm, d), lambda i, f: (i, 0)),
            pl.BlockSpec((None, 1, d), lambda i, f: (l, 0, 0)),
            pl.BlockSpec((None, 1, d), mod_idx(3)),
            pl.BlockSpec((None, 1, d), mod_idx(4)),
            pl.BlockSpec((None, 1, d), mod_idx(5)),
            pl.BlockSpec((None, d, tf), lambda i, f: (l, 0, f)),
            pl.BlockSpec((None, d, tf), lambda i, f: (l, 0, nf + f)),
            pl.BlockSpec((None, tf, d), lambda i, f: (l, f, 0)),
            pl.BlockSpec((1, d), lambda i, f: (0, 0)),
        ],
        out_specs=pl.BlockSpec((tm, d), lambda i, f: (i, 0)),
        out_shape=jax.ShapeDtypeStruct((rows, d), F32),
        scratch_shapes=[pltpu.VMEM((tm, d), BF16)],
        compiler_params=_cparams(("parallel", "arbitrary")),
        name="ffn",
    )(xt, norm_g, modr, modr, modr, w_gu, w_gu, w_dn, final_g)


def _rope_tables(seq, pad_rows):
    rows = seq // GRID_W
    inv_freq = ROPE_THETA ** (-jnp.arange(0, ROPE_AXIS, 2, dtype=F32) / ROPE_AXIS)
    row = jnp.repeat(jnp.arange(rows, dtype=F32), GRID_W)
    col = jnp.tile(jnp.arange(GRID_W, dtype=F32), rows)
    ar = row[:, None] * inv_freq
    ac = col[:, None] * inv_freq
    cos = jnp.concatenate([jnp.cos(ar), jnp.cos(ar), jnp.cos(ac), jnp.cos(ac)], axis=1)
    sin = jnp.concatenate([-jnp.sin(ar), jnp.sin(ar), -jnp.sin(ac), jnp.sin(ac)], axis=1)
    cos = jnp.concatenate([cos, jnp.ones((pad_rows, HEAD_DIM), F32)], axis=0)
    sin = jnp.concatenate([sin, jnp.zeros((pad_rows, HEAD_DIM), F32)], axis=0)
    return cos, sin


def kernel(x, c, ctx, c_ctx, w_ada, b_ada, norm_mix_g, norm_ffn_g, w_in, attn_sink, pool_w, pool_scale,
           sgu_norm_g, sgu_w, sgu_b, w_out, w_gate_up, w_down, final_norm_g):
    bsz, seq, d = x.shape
    ctx_len = ctx.shape[1]
    depth = w_in.shape[0]
    n_lat_rows = bsz * seq
    n_rows = n_lat_rows + bsz * ctx_len

    cond = jnp.concatenate([c, c_ctx[None, :], jnp.zeros((MOD_ROWS - bsz - 1, d), F32)], axis=0)
    mod = _ada_table(cond, w_ada, b_ada)
    modr = mod.reshape(depth * MOD_ROWS * MOD_CHUNKS, 1, d)

    cos, sin = _rope_tables(seq, ROW_TILE)
    w_in_b = w_in.astype(BF16)
    w_out_b = w_out.astype(BF16)
    w_gu_b = w_gate_up.astype(BF16)
    w_dn_b = w_down.astype(BF16)
    pool_w_b = pool_w.astype(BF16)
    sgu_w_b = sgu_w.astype(BF16)
    norm_mix = norm_mix_g.reshape(depth, 1, d)
    norm_ffn = norm_ffn_g.reshape(depth, 1, d)
    final_g = final_norm_g.reshape(1, d)

    xs = (x.reshape(n_lat_rows, d), ctx.reshape(bsz * ctx_len, d))
    for l in range(depth):
        last = l == depth - 1
        rows = n_lat_rows if last else n_rows
        q, kv, puz = _inproj(xs, norm_mix, modr, cos, sin, w_in_b, l,
                             n_rows=n_rows, n_lat_rows=n_lat_rows, seq=seq)
        sgu_b_full = jnp.broadcast_to(sgu_b[l][:, :, None], sgu_b.shape[1:] + (HEAD_DIM,))
        mix = _mixer(q, kv, puz, attn_sink[l], pool_w_b[l], pool_scale[l].reshape(1, -1),
                     sgu_norm_g[l].reshape(1, -1), sgu_w_b[l], sgu_b_full,
                     n_lat_rows=n_lat_rows, seq=seq, ctx_len=ctx_len, with_ctx=not last)
        xt = _outproj(mix, xs, modr, w_out_b, l, seq=seq, n_lat_rows=n_lat_rows)
        xt = _ffn(xt, norm_ffn, modr, w_gu_b, w_dn_b, final_g, l, rows=rows, seq=seq,
                  n_lat_rows=n_lat_rows, final_norm=last)
        xs = (xt,)
    return xt.reshape(bsz, seq, d)
```

```python
import functools

import jax
import jax.numpy as jnp
from jax import lax
from jax.experimental import pallas as pl
from jax.experimental.pallas import tpu as pltpu

F32 = jnp.float32
BF16 = jnp.bfloat16

GRID_W = 64
HEAD_DIM = 128
GROUP = 4
WINDOW_BLOCK = 128
ROPE_THETA = 10000.0
ROPE_AXIS = HEAD_DIM // 2
POOL_WINDOWS = (2, 4, 8, 16)
POOL_HALO = 8
EPS = 1e-6
NEG = -1e30
LOG2E = 1.4426950408889634

MOD_ROWS = 8
MOD_CHUNKS = 6

LANES = 128
BF16_SUBLANES = 16
VMEM_LIMIT = 56 * 1024 * 1024
FFN_VMEM_LIMIT = 62 * 1024 * 1024
ROW_TILE = 512
FFN_ROW_TILE = 1024
FFN_COL_TILE = 512
NORM_CHUNK = 32
NORM_UNROLL = 4


def _cparams(sem, vmem_limit=VMEM_LIMIT):
    return pltpu.CompilerParams(dimension_semantics=sem, vmem_limit_bytes=vmem_limit)


def _dot(a, b):
    return jnp.dot(a, b, preferred_element_type=F32)


def _mod_index(l, group, chunk):
    return (l * MOD_ROWS + group) * MOD_CHUNKS + chunk


def _ada_kernel(c_ref, w_ref, b_ref, o_ref):
    cv = c_ref[...]
    a = (cv * jax.nn.sigmoid(cv)).astype(BF16)
    o_ref[...] = _dot(a, w_ref[...].astype(BF16)) + b_ref[...]


def _ada_table(cond, w_ada, b_ada):
    depth, d, n = w_ada.shape
    tn = 1024
    return pl.pallas_call(
        _ada_kernel,
        grid=(depth, n // tn),
        in_specs=[
            pl.BlockSpec((MOD_ROWS, d), lambda l, j: (0, 0)),
            pl.BlockSpec((None, d, tn), lambda l, j: (l, 0, j)),
            pl.BlockSpec((None, 1, tn), lambda l, j: (l, 0, j)),
        ],
        out_specs=pl.BlockSpec((None, MOD_ROWS, tn), lambda l, j: (l, 0, j)),
        out_shape=jax.ShapeDtypeStruct((depth, MOD_ROWS, n), F32),
        compiler_params=_cparams(("parallel", "parallel")),
        name="ada_table",
    )(cond, w_ada, b_ada.reshape(depth, 1, n))


def _modulated_norm(x, gain, shift):
    ms = jnp.mean(x * x, axis=-1, keepdims=True)
    return x * lax.rsqrt(ms + EPS) * gain + shift


def _pick_rows(lat_ref, ctx_ref, rows, is_lat_tile):
    if ctx_ref is None:
        return lat_ref[rows, :]
    a = lat_ref[rows, :]
    sel = jnp.full(a.shape, is_lat_tile, jnp.int32) > 0
    return jnp.where(sel, a, ctx_ref[rows, :])


def _inproj_kernel(*refs, n_lat_tiles, q_w, kv_w, split_src):
    if split_src:
        xa_ref, xb_ref = refs[:2]
        refs = refs[2:]
    else:
        xa_ref, xb_ref = refs[0], None
        refs = refs[1:]
    g_ref, sh_ref, sc_ref, cos_ref, sin_ref, w_ref, q_ref, kv_ref, puz_ref, hb_ref = refs
    i = pl.program_id(0)
    tm = hb_ref.shape[0]
    is_lat = (i < n_lat_tiles).astype(jnp.int32)

    gain = g_ref[...] * (1.0 + sc_ref[...])
    shift = sh_ref[...]

    def body(r, carry):
        rows = pl.ds(pl.multiple_of(r * NORM_CHUNK, NORM_CHUNK), NORM_CHUNK)
        xs = _pick_rows(xa_ref, xb_ref, rows, is_lat)
        hb_ref[rows, :] = _modulated_norm(xs, gain, shift).astype(BF16)
        return carry
    lax.fori_loop(0, tm // NORM_CHUNK, body, 0, unroll=NORM_UNROLL)

    hb = hb_ref[...]
    pq = _dot(hb, w_ref[:, :q_w])
    pkv = _dot(hb, w_ref[:, q_w:q_w + 2 * kv_w])
    puz_ref[...] = _dot(hb, w_ref[:, q_w + 2 * kv_w:])

    cos = cos_ref[...]
    sin = sin_ref[...]
    lane = lax.broadcasted_iota(jnp.int32, (tm, HEAD_DIM), 1)
    first_half = (lane % ROPE_AXIS) < (ROPE_AXIS // 2)

    def rope(t):
        rot = jnp.where(first_half,
                        pltpu.roll(t, HEAD_DIM - ROPE_AXIS // 2, 1),
                        pltpu.roll(t, ROPE_AXIS // 2, 1))
        return t * cos + rot * sin

    for hd in range(q_w // HEAD_DIM):
        sl = slice(hd * HEAD_DIM, (hd + 1) * HEAD_DIM)
        q_ref[:, sl] = rope(pq[:, sl]).astype(BF16)
    for hd in range(kv_w // HEAD_DIM):
        sl = slice(hd * HEAD_DIM, (hd + 1) * HEAD_DIM)
        kv_ref[:, sl] = rope(pkv[:, sl]).astype(BF16)
    kv_ref[:, kv_w:] = pkv[:, kv_w:].astype(BF16)


def _row_sources(x_lat, x_ctx, tm, n_lat_tiles):
    d = x_lat.shape[1]
    lat_map = lambda i: (jnp.minimum(i, n_lat_tiles - 1), 0)
    ctx_map = lambda i: (jnp.maximum(i - n_lat_tiles, 0), 0)
    return [pl.BlockSpec((tm, d), lat_map), pl.BlockSpec((tm, d), ctx_map)], [x_lat, x_ctx]


def _inproj(xs, norm_g, modr, cos, sin, w_in, l, *, n_rows, n_lat_rows, seq):
    d = xs[0].shape[1]
    in_w = w_in.shape[-1]
    q_w = d // 2
    kv_w = q_w // GROUP
    puz_w = in_w - q_w - 2 * kv_w
    tm = ROW_TILE
    tiles_per_seq = seq // tm
    n_lat_tiles = n_lat_rows // tm
    n_groups = n_lat_tiles // tiles_per_seq
    split_src = len(xs) == 2

    def mod_idx(k):
        return lambda i: (_mod_index(l, jnp.minimum(i // tiles_per_seq, n_groups), k), 0, 0)

    def rope_idx(i):
        return (jnp.where(i < n_lat_tiles, i % tiles_per_seq, tiles_per_seq), 0)

    if split_src:
        x_specs, x_args = _row_sources(xs[0], xs[1], tm, n_lat_tiles)
    else:
        x_specs, x_args = [pl.BlockSpec((tm, d), lambda i: (i, 0))], [xs[0]]

    kern = functools.partial(_inproj_kernel, n_lat_tiles=n_lat_tiles, q_w=q_w, kv_w=kv_w,
                             split_src=split_src)
    return pl.pallas_call(
        kern,
        grid=(n_rows // tm,),
        in_specs=x_specs + [
            pl.BlockSpec((None, 1, d), lambda i: (l, 0, 0)),
            pl.BlockSpec((None, 1, d), mod_idx(0)),
            pl.BlockSpec((None, 1, d), mod_idx(1)),
            pl.BlockSpec((tm, HEAD_DIM), rope_idx),
            pl.BlockSpec((tm, HEAD_DIM), rope_idx),
            pl.BlockSpec((d, in_w), lambda i: (0, 0), pipeline_mode=pl.Buffered(1)),
        ],
        out_specs=[
            pl.BlockSpec((tm, q_w), lambda i: (i, 0)),
            pl.BlockSpec((tm, 2 * kv_w), lambda i: (i, 0)),
            pl.BlockSpec((tm, puz_w), lambda i: (i, 0)),
        ],
        out_shape=[
            jax.ShapeDtypeStruct((n_rows, q_w), BF16),
            jax.ShapeDtypeStruct((n_rows, 2 * kv_w), BF16),
            jax.ShapeDtypeStruct((n_rows, puz_w), F32),
        ],
        scratch_shapes=[pltpu.VMEM((tm, d), BF16)],
        compiler_params=_cparams(("parallel",)),
        name="in_proj",
    )(*x_args, norm_g, modr, modr, cos, sin, w_in)


def _gelu(x):
    return 0.5 * x * (1.0 + lax.erf(x * (0.5 ** 0.5)))


def _mixer_kernel(sink_ref, q_ref, kvp_ref, kvc_ref, kvn_ref, kvx_ref, puz_ref,
                  php_ref, phn_ref, pw_ref, ps_ref, sg_ref, sw_ref, sb_ref,
                  o_ref, ext_ref, *, n_lat_blocks, lat_blocks_per_seq, ctx_blocks_per_seq,
                  n_kv_heads, pool_w, sgu_w):
    blk = WINDOW_BLOCK
    hd = HEAD_DIM
    i = pl.program_id(0)
    is_lat = i < n_lat_blocks
    n = jnp.where(is_lat, i % lat_blocks_per_seq, (i - n_lat_blocks) % ctx_blocks_per_seq)
    nblk = jnp.where(is_lat, lat_blocks_per_seq, ctx_blocks_per_seq)
    has_prev = n > 0
    has_next = n < nblk - 1
    kv_w = n_kv_heads * hd
    q_w = kv_w * GROUP
    logit_scale = hd ** -0.5 * LOG2E

    def attend(keys, vals, tile_masks, kvh):
        nk = keys.shape[0]
        qs = jnp.concatenate(
            [q_ref[:, (kvh * GROUP + g) * hd:(kvh * GROUP + g + 1) * hd] for g in range(GROUP)],
            axis=0)
        s = lax.dot_general(qs, keys, (((1,), (1,)), ((), ())),
                            preferred_element_type=F32) * logit_scale
        vals1 = jnp.concatenate([vals, jnp.ones((nk, hd), BF16)], axis=1)
        es, sink_terms = [], []
        for g in range(GROUP):
            tiles = []
            for t, mask in enumerate(tile_masks):
                st = s[g * blk:(g + 1) * blk, t * blk:(t + 1) * blk]
                tiles.append(st if mask is None else jnp.where(mask, st, NEG))
            mt = tiles[0]
            for st in tiles[1:]:
                mt = jnp.maximum(mt, st)
            snk = sink_ref[kvh * GROUP + g] * LOG2E
            m = jnp.maximum(jnp.max(mt, axis=-1, keepdims=True), snk)
            es.append(jnp.concatenate([jnp.exp2(st - m).astype(BF16) for st in tiles], axis=1))
            sink_terms.append(jnp.exp2(snk - m))
        o = _dot(jnp.concatenate(es, axis=0), vals1)
        for g in range(GROUP):
            h = kvh * GROUP + g
            og = o[g * blk:(g + 1) * blk]
            o_ref[:, h * hd:(h + 1) * hd] = (og[:, :hd] / (og[:, hd:] + sink_terms[g])).astype(BF16)

    @pl.when(is_lat)
    def _():
        qi = lax.broadcasted_iota(jnp.int32, (blk, blk), 0)
        kj = lax.broadcasted_iota(jnp.int32, (blk, blk), 1)
        mask_prev = kj >= qi + jnp.where(has_prev, 0, blk)
        mask_next = kj <= qi - jnp.where(has_next, 0, blk)
        masks = [mask_prev, None, mask_next] + [None] * (kvx_ref.shape[0] // blk)
        for kvh in range(n_kv_heads):
            ks = slice(kvh * hd, (kvh + 1) * hd)
            vs = slice(kv_w + kvh * hd, kv_w + (kvh + 1) * hd)
            keys = jnp.concatenate([kvp_ref[:, ks], kvc_ref[:, ks], kvn_ref[:, ks], kvx_ref[:, ks]], axis=0)
            vals = jnp.concatenate([kvp_ref[:, vs], kvc_ref[:, vs], kvn_ref[:, vs], kvx_ref[:, vs]], axis=0)
            attend(keys, vals, masks, kvh)

    @pl.when(jnp.logical_not(is_lat))
    def _():
        masks = [None] * (kvx_ref.shape[0] // blk)
        for kvh in range(n_kv_heads):
            attend(kvx_ref[:, kvh * hd:(kvh + 1) * hd],
                   kvx_ref[:, kv_w + kvh * hd:kv_w + (kvh + 1) * hd], masks, kvh)

    halo = POOL_HALO

    @pl.when(has_prev)
    def _():
        ext_ref[0:halo, :] = php_ref[...]

    @pl.when(jnp.logical_not(has_prev))
    def _():
        ext_ref[0:halo, :] = jnp.zeros((halo, pool_w), F32)

    @pl.when(has_next)
    def _():
        ext_ref[halo + blk:, :] = phn_ref[...]

    @pl.when(jnp.logical_not(has_next))
    def _():
        ext_ref[halo + blk:, :] = jnp.zeros((halo, pool_w), F32)

    ext_ref[halo:halo + blk, :] = puz_ref[:, :pool_w]
    pos = n * blk + lax.broadcasted_iota(jnp.int32, (blk, hd), 0)
    seq_len = nblk * blk
    for g, win in enumerate(POOL_WINDOWS):
        cs = slice(g * hd, (g + 1) * hd)
        half = win // 2
        acc = ext_ref[halo - half:halo - half + blk, cs]
        for d in range(-half + 1, half):
            acc = acc + ext_ref[halo + d:halo + d + blk, cs]
        cnt = (jnp.minimum(pos + half, seq_len) - jnp.maximum(pos - half, 0)).astype(F32)
        pooled = acc / cnt - puz_ref[:, cs]
        y = _dot(pooled.astype(BF16), pw_ref[g]) * ps_ref[:, cs]
        o_ref[:, q_w + g * hd:q_w + (g + 1) * hd] = y.astype(BF16)

    for g in range(sgu_w // hd):
        cs = slice(g * hd, (g + 1) * hd)
        u = _gelu(puz_ref[:, pool_w + g * hd:pool_w + (g + 1) * hd])
        z = _gelu(puz_ref[:, pool_w + sgu_w + g * hd:pool_w + sgu_w + (g + 1) * hd])
        mu = jnp.mean(z, axis=-1, keepdims=True)
        zc = z - mu
        var = jnp.mean(zc * zc, axis=-1, keepdims=True)
        zn = zc * lax.rsqrt(var + EPS) * sg_ref[:, cs]
        mixed = _dot(sw_ref[g], zn.astype(BF16)) + sb_ref[g]
        o_ref[:, q_w + pool_w + g * hd:q_w + pool_w + (g + 1) * hd] = (u * mixed).astype(BF16)


def _mixer(q, kv, puz, sink, pool_w_l, pool_scale_l, sgu_g_l, sgu_w_l, sgu_b_l, *,
           n_lat_rows, seq, ctx_len, with_ctx):
    t = q.shape[0]
    blk = WINDOW_BLOCK
    q_w = q.shape[1]
    kv_w = kv.shape[1] // 2
    pool_w = pool_scale_l.shape[-1]
    sgu_w = sgu_g_l.shape[-1]
    d_mix = q_w + pool_w + sgu_w
    n_lat = n_lat_rows // blk
    lat_per = seq // blk
    ctx_per = ctx_len // blk
    rows = t if with_ctx else n_lat_rows
    hpb = blk // POOL_HALO

    def seq_pos(i):
        lat = i < n_lat
        n = jnp.where(lat, i % lat_per, (i - n_lat) % ctx_per)
        nblk = jnp.where(lat, lat_per, ctx_per)
        return n, nblk

    def prev_idx(i):
        n, _ = seq_pos(i)
        return jnp.where(n > 0, i - 1, i)

    def next_idx(i):
        n, nblk = seq_pos(i)
        return jnp.where(n < nblk - 1, i + 1, i)

    def ctx_idx(i):
        b = jnp.where(i < n_lat, i // lat_per, (i - n_lat) // ctx_per)
        return n_lat_rows // ctx_len + b

    def halo_prev_idx(i):
        n, _ = seq_pos(i)
        return jnp.where(n > 0, i * hpb - 1, i * hpb)

    def halo_next_idx(i):
        n, nblk = seq_pos(i)
        return jnp.where(n < nblk - 1, (i + 1) * hpb, i * hpb)

    kern = functools.partial(
        _mixer_kernel, n_lat_blocks=n_lat, lat_blocks_per_seq=lat_per,
        ctx_blocks_per_seq=ctx_per, n_kv_heads=kv_w // HEAD_DIM, pool_w=pool_w, sgu_w=sgu_w)
    n_pool = pool_w // HEAD_DIM
    n_sgu = sgu_w // HEAD_DIM
    return pl.pallas_call(
        kern,
        grid=(rows // blk,),
        in_specs=[
            pl.BlockSpec(memory_space=pltpu.SMEM),
            pl.BlockSpec((blk, q_w), lambda i: (i, 0)),
            pl.BlockSpec((blk, 2 * kv_w), lambda i: (prev_idx(i), 0)),
            pl.BlockSpec((blk, 2 * kv_w), lambda i: (i, 0)),
            pl.BlockSpec((blk, 2 * kv_w), lambda i: (next_idx(i), 0)),
            pl.BlockSpec((ctx_len, 2 * kv_w), lambda i: (ctx_idx(i), 0)),
            pl.BlockSpec((blk, pool_w + 2 * sgu_w), lambda i: (i, 0)),
            pl.BlockSpec((POOL_HALO, pool_w), lambda i: (halo_prev_idx(i), 0)),
            pl.BlockSpec((POOL_HALO, pool_w), lambda i: (halo_next_idx(i), 0)),
            pl.BlockSpec((n_pool, HEAD_DIM, HEAD_DIM), lambda i: (0, 0, 0)),
            pl.BlockSpec((1, pool_w), lambda i: (0, 0)),
            pl.BlockSpec((1, sgu_w), lambda i: (0, 0)),
            pl.BlockSpec((n_sgu, blk, blk), lambda i: (0, 0, 0)),
            pl.BlockSpec((n_sgu, blk, HEAD_DIM), lambda i: (0, 0, 0)),
        ],
        out_specs=pl.BlockSpec((blk, d_mix), lambda i: (i, 0)),
        out_shape=jax.ShapeDtypeStruct((rows, d_mix), BF16),
        scratch_shapes=[pltpu.VMEM((blk + 2 * POOL_HALO, pool_w), F32)],
        compiler_params=_cparams(("parallel",)),
        name="mixer",
    )(sink, q, kv, kv, kv, kv, puz, puz, puz, pool_w_l, pool_scale_l, sgu_g_l, sgu_w_l, sgu_b_l)


def _outproj_kernel(*refs, n_lat_tiles, split_src):
    if split_src:
        mix_ref, xa_ref, xb_ref, gate_ref, w_ref, o_ref = refs
    else:
        mix_ref, xa_ref, gate_ref, w_ref, o_ref = refs
        xb_ref = None
    is_lat = (pl.program_id(0) < n_lat_tiles).astype(jnp.int32)
    x = _pick_rows(xa_ref, xb_ref, slice(None), is_lat)
    o_ref[...] = x + gate_ref[...] * _dot(mix_ref[...], w_ref[...])


def _outproj(mix, xs, modr, w_out, l, *, seq, n_lat_rows):
    rows, d_mix = mix.shape
    d = xs[0].shape[1]
    tm = ROW_TILE
    tiles_per_seq = seq // tm
    n_lat_tiles = n_lat_rows // tm
    n_groups = n_lat_rows // seq
    split_src = len(xs) == 2
    if split_src:
        x_specs, x_args = _row_sources(xs[0], xs[1], tm, n_lat_tiles)
    else:
        x_specs, x_args = [pl.BlockSpec((tm, d), lambda i: (i, 0))], [xs[0]]
    kern = functools.partial(_outproj_kernel, n_lat_tiles=n_lat_tiles, split_src=split_src)
    return pl.pallas_call(
        kern,
        grid=(rows // tm,),
        in_specs=[pl.BlockSpec((tm, d_mix), lambda i: (i, 0))] + x_specs + [
            pl.BlockSpec((None, 1, d),
                         lambda i: (_mod_index(l, jnp.minimum(i // tiles_per_seq, n_groups), 2), 0, 0)),
            pl.BlockSpec((d_mix, d), lambda i: (0, 0), pipeline_mode=pl.Buffered(1)),
        ],
        out_specs=pl.BlockSpec((tm, d), lambda i: (i, 0)),
        out_shape=jax.ShapeDtypeStruct((rows, d), F32),
        compiler_params=_cparams(("parallel",)),
        name="out_proj",
    )(mix, *x_args, modr, w_out)


def _ffn_kernel(*refs, final_norm, n_cast):
    (x_ref, g_ref, sh_ref, sc_ref, gate_ref, wg_ref, wu_ref, wd_ref, fg_ref), refs = refs[:9], refs[9:]
    cast_src, refs = refs[:n_cast], refs[n_cast:]
    o_ref, refs = refs[0], refs[1:]
    cast_dst, (xn_ref,) = refs[:n_cast], refs[n_cast:]
    f = pl.program_id(1)
    nf = pl.num_programs(1)
    tm, d = x_ref.shape
    n_chunks = tm // NORM_CHUNK

    @pl.when(f == 0)
    def _():
        gain = g_ref[...] * (1.0 + sc_ref[...])
        shift = sh_ref[...]

        def body(r, carry):
            rows = pl.ds(pl.multiple_of(r * NORM_CHUNK, NORM_CHUNK), NORM_CHUNK)
            h = _modulated_norm(x_ref[rows, :], gain, shift)
            xn_ref[rows, :] = h.astype(BF16)
            o_ref[rows, :] = jnp.zeros((NORM_CHUNK, d), F32)
            return carry
        lax.fori_loop(0, n_chunks, body, 0, unroll=NORM_UNROLL)

    for src, dst in zip(cast_src, cast_dst):
        dst[...] = src[...].astype(BF16)

    xn = xn_ref[...]
    gg = _dot(xn, wg_ref[...])
    uu = _dot(xn, wu_ref[...])
    act = (gg * jax.nn.sigmoid(gg) * uu).astype(BF16)
    for cidx in range(d // FFN_COL_TILE):
        cs = slice(cidx * FFN_COL_TILE, (cidx + 1) * FFN_COL_TILE)
        o_ref[:, cs] += _dot(act, wd_ref[:, cs])

    @pl.when(f == nf - 1)
    def _():
        def body(r, carry):
            rows = pl.ds(pl.multiple_of(r * NORM_CHUNK, NORM_CHUNK), NORM_CHUNK)
            y = x_ref[rows, :] + gate_ref[...] * o_ref[rows, :]
            if final_norm:
                ms = jnp.mean(y * y, axis=-1, keepdims=True)
                y = y * lax.rsqrt(ms + EPS) * fg_ref[...]
            o_ref[rows, :] = y
            return carry
        lax.fori_loop(0, n_chunks, body, 0, unroll=NORM_UNROLL)


def _cast_chunks(weights, l_next, n_i, n_f):
    def split(n, unit, limit):
        return max(k for k in range(1, limit + 1) if n % (k * unit) == 0)

    in_specs, out_specs, out_shapes = [], [], []
    for w in weights:
        _, r, c = w.shape
        rows_i = (split(r, BF16_SUBLANES, n_i), split(c, LANES, n_f))
        rows_f = (split(r, BF16_SUBLANES, n_f), split(c, LANES, n_i))
        if rows_i[0] * rows_i[1] >= rows_f[0] * rows_f[1]:
            kr, kc = rows_i
            idx = lambda i, f, kr=kr, kc=kc: (
                jnp.minimum(i, kr - 1), jnp.where(i < kr, jnp.minimum(f, kc - 1), kc - 1))
        else:
            kr, kc = rows_f
            idx = lambda i, f, kr=kr, kc=kc: (
                jnp.where(i < kc, jnp.minimum(f, kr - 1), kr - 1), jnp.minimum(i, kc - 1))
        blk = (r // kr, c // kc)
        in_specs.append(pl.BlockSpec((None,) + blk, lambda i, f, idx=idx: (l_next,) + idx(i, f)))
        out_specs.append(pl.BlockSpec(blk, idx))
        out_shapes.append(jax.ShapeDtypeStruct((r, c), BF16))
    return in_specs, out_specs, out_shapes


def _ffn(xt, norm_g, modr, w_gu, w_dn, final_g, l, *, rows, seq, n_lat_rows, final_norm, cast_next=()):
    d = xt.shape[1]
    d_ff = w_dn.shape[0]
    tm = FFN_ROW_TILE
    tf = FFN_COL_TILE
    nf = d_ff // tf
    tiles_per_seq = seq // tm
    n_groups = n_lat_rows // seq

    def mod_idx(k):
        return lambda i, f: (_mod_index(l, jnp.minimum(i // tiles_per_seq, n_groups), k), 0, 0)

    cast_in, cast_out, cast_shapes = _cast_chunks(cast_next, l + 1, rows // tm, nf)
    kern = functools.partial(_ffn_kernel, final_norm=final_norm, n_cast=len(cast_next))
    outs = pl.pallas_call(
        kern,
        grid=(rows // tm, nf),
        in_specs=[
            pl.BlockSpec((tm, d), lambda i, f: (i, 0)),
            pl.BlockSpec((None, 1, d), lambda i, f: (l, 0, 0)),
            pl.BlockSpec((None, 1, d), mod_idx(3)),
            pl.BlockSpec((None, 1, d), mod_idx(4)),
            pl.BlockSpec((None, 1, d), mod_idx(5)),
            pl.BlockSpec((d, tf), lambda i, f: (0, f)),
            pl.BlockSpec((d, tf), lambda i, f: (0, nf + f)),
            pl.BlockSpec((tf, d), lambda i, f: (f, 0)),
            pl.BlockSpec((1, d), lambda i, f: (0, 0)),
        ] + cast_in,
        out_specs=[pl.BlockSpec((tm, d), lambda i, f: (i, 0))] + cast_out,
        out_shape=[jax.ShapeDtypeStruct((rows, d), F32)] + cast_shapes,
        scratch_shapes=[pltpu.VMEM((tm, d), BF16)],
        compiler_params=_cparams(("arbitrary", "arbitrary"), FFN_VMEM_LIMIT),
        name="ffn",
    )(xt, norm_g, modr, modr, modr, w_gu, w_gu, w_dn, final_g, *cast_next)
    return outs[0], tuple(outs[1:])


def _rope_tables(seq, pad_rows):
    rows = seq // GRID_W
    inv_freq = ROPE_THETA ** (-jnp.arange(0, ROPE_AXIS, 2, dtype=F32) / ROPE_AXIS)
    row = jnp.repeat(jnp.arange(rows, dtype=F32), GRID_W)
    col = jnp.tile(jnp.arange(GRID_W, dtype=F32), rows)
    ar = row[:, None] * inv_freq
    ac = col[:, None] * inv_freq
    cos = jnp.concatenate([jnp.cos(ar), jnp.cos(ar), jnp.cos(ac), jnp.cos(ac)], axis=1)
    sin = jnp.concatenate([-jnp.sin(ar), jnp.sin(ar), -jnp.sin(ac), jnp.sin(ac)], axis=1)
    cos = jnp.concatenate([cos, jnp.ones((pad_rows, HEAD_DIM), F32)], axis=0)
    sin = jnp.concatenate([sin, jnp.zeros((pad_rows, HEAD_DIM), F32)], axis=0)
    return cos, sin


def kernel(x, c, ctx, c_ctx, w_ada, b_ada, norm_mix_g, norm_ffn_g, w_in, attn_sink, pool_w, pool_scale,
           sgu_norm_g, sgu_w, sgu_b, w_out, w_gate_up, w_down, final_norm_g):
    bsz, seq, d = x.shape
    ctx_len = ctx.shape[1]
    depth = w_in.shape[0]
    n_lat_rows = bsz * seq
    n_rows = n_lat_rows + bsz * ctx_len

    cond = jnp.concatenate([c, c_ctx[None, :], jnp.zeros((MOD_ROWS - bsz - 1, d), F32)], axis=0)
    mod = _ada_table(cond, w_ada, b_ada)
    modr = mod.reshape(depth * MOD_ROWS * MOD_CHUNKS, 1, d)

    cos, sin = _rope_tables(seq, ROW_TILE)
    big_weights = (w_in, w_out, w_gate_up, w_down)
    w_in_b, w_out_b, w_gu_b, w_dn_b = (w[0].astype(BF16) for w in big_weights)
    pool_w_b = pool_w.astype(BF16)
    sgu_w_b = sgu_w.astype(BF16)
    norm_mix = norm_mix_g.reshape(depth, 1, d)
    norm_ffn = norm_ffn_g.reshape(depth, 1, d)
    final_g = final_norm_g.reshape(1, d)

    xs = (x.reshape(n_lat_rows, d), ctx.reshape(bsz * ctx_len, d))
    for l in range(depth):
        last = l == depth - 1
        rows = n_lat_rows if last else n_rows
        q, kv, puz = _inproj(xs, norm_mix, modr, cos, sin, w_in_b, l,
                             n_rows=n_rows, n_lat_rows=n_lat_rows, seq=seq)
        sgu_b_full = jnp.broadcast_to(sgu_b[l][:, :, None], sgu_b.shape[1:] + (HEAD_DIM,))
        mix = _mixer(q, kv, puz, attn_sink[l], pool_w_b[l], pool_scale[l].reshape(1, -1),
                     sgu_norm_g[l].reshape(1, -1), sgu_w_b[l], sgu_b_full,
                     n_lat_rows=n_lat_rows, seq=seq, ctx_len=ctx_len, with_ctx=not last)
        xt = _outproj(mix, xs, modr, w_out_b, l, seq=seq, n_lat_rows=n_lat_rows)
        xt, next_w = _ffn(xt, norm_ffn, modr, w_gu_b, w_dn_b, final_g, l, rows=rows, seq=seq,
                          n_lat_rows=n_lat_rows, final_norm=last,
                          cast_next=() if last else big_weights)
        if not last:
            w_in_b, w_out_b, w_gu_b, w_dn_b = next_w
        xs = (xt,)
    return xt.reshape(bsz, seq, d)
```

```python
import functools

import jax
import jax.numpy as jnp
from jax import lax
from jax.experimental import pallas as pl
from jax.experimental.pallas import tpu as pltpu

F32 = jnp.float32
BF16 = jnp.bfloat16

GRID_W = 64
HEAD_DIM = 128
GROUP = 4
WINDOW_BLOCK = 128
ROPE_THETA = 10000.0
ROPE_AXIS = HEAD_DIM // 2
POOL_WINDOWS = (2, 4, 8, 16)
POOL_HALO = 8
EPS = 1e-6
NEG = -1e30
LOG2E = 1.4426950408889634

MOD_ROWS = 8
MOD_CHUNKS = 6

LANES = 128
BF16_SUBLANES = 16
VMEM_LIMIT = 56 * 1024 * 1024
FFN_VMEM_LIMIT = 62 * 1024 * 1024
ROW_TILE = 512
FFN_ROW_TILE = 1024
FFN_COL_TILE = 512
NORM_CHUNK = 32
NORM_UNROLL = 4


def _cparams(sem, vmem_limit=VMEM_LIMIT):
    return pltpu.CompilerParams(dimension_semantics=sem, vmem_limit_bytes=vmem_limit)


def _dot(a, b):
    return jnp.dot(a, b, preferred_element_type=F32)


def _mod_index(l, group, chunk):
    return (l * MOD_ROWS + group) * MOD_CHUNKS + chunk


def _ada_kernel(c_ref, w_ref, b_ref, o_ref):
    cv = c_ref[...]
    a = (cv * jax.nn.sigmoid(cv)).astype(BF16)
    o_ref[...] = _dot(a, w_ref[...].astype(BF16)) + b_ref[...]


def _ada_table(cond, w_ada, b_ada):
    depth, d, n = w_ada.shape
    tn = 1024
    return pl.pallas_call(
        _ada_kernel,
        grid=(depth, n // tn),
        in_specs=[
            pl.BlockSpec((MOD_ROWS, d), lambda l, j: (0, 0)),
            pl.BlockSpec((None, d, tn), lambda l, j: (l, 0, j)),
            pl.BlockSpec((None, 1, tn), lambda l, j: (l, 0, j)),
        ],
        out_specs=pl.BlockSpec((None, MOD_ROWS, tn), lambda l, j: (l, 0, j)),
        out_shape=jax.ShapeDtypeStruct((depth, MOD_ROWS, n), F32),
        compiler_params=_cparams(("parallel", "parallel")),
        name="ada_table",
    )(cond, w_ada, b_ada.reshape(depth, 1, n))


def _modulated_norm(x, gain, shift):
    ms = jnp.mean(x * x, axis=-1, keepdims=True)
    return x * lax.rsqrt(ms + EPS) * gain + shift


def _pick_rows(lat_ref, ctx_ref, rows, is_lat_tile):
    if ctx_ref is None:
        return lat_ref[rows, :]
    a = lat_ref[rows, :]
    sel = jnp.full(a.shape, is_lat_tile, jnp.int32) > 0
    return jnp.where(sel, a, ctx_ref[rows, :])


def _inproj_kernel(*refs, n_lat_tiles, q_w, kv_w, split_src):
    if split_src:
        xa_ref, xb_ref = refs[:2]
        refs = refs[2:]
    else:
        xa_ref, xb_ref = refs[0], None
        refs = refs[1:]
    g_ref, sh_ref, sc_ref, cos_ref, sin_ref, w_ref, q_ref, kv_ref, puz_ref, hb_ref = refs
    i = pl.program_id(0)
    tm = hb_ref.shape[0]
    is_lat = (i < n_lat_tiles).astype(jnp.int32)

    gain = g_ref[...] * (1.0 + sc_ref[...])
    shift = sh_ref[...]

    def body(r, carry):
        rows = pl.ds(pl.multiple_of(r * NORM_CHUNK, NORM_CHUNK), NORM_CHUNK)
        xs = _pick_rows(xa_ref, xb_ref, rows, is_lat)
        hb_ref[rows, :] = _modulated_norm(xs, gain, shift).astype(BF16)
        return carry
    lax.fori_loop(0, tm // NORM_CHUNK, body, 0, unroll=NORM_UNROLL)

    hb = hb_ref[...]
    pq = _dot(hb, w_ref[:, :q_w])
    pkv = _dot(hb, w_ref[:, q_w:q_w + 2 * kv_w])
    puz_ref[...] = _dot(hb, w_ref[:, q_w + 2 * kv_w:])

    cos = cos_ref[...]
    sin = sin_ref[...]
    lane = lax.broadcasted_iota(jnp.int32, (tm, HEAD_DIM), 1)
    first_half = (lane % ROPE_AXIS) < (ROPE_AXIS // 2)

    def rope(t):
        rot = jnp.where(first_half,
                        pltpu.roll(t, HEAD_DIM - ROPE_AXIS // 2, 1),
                        pltpu.roll(t, ROPE_AXIS // 2, 1))
        return t * cos + rot * sin

    for hd in range(q_w // HEAD_DIM):
        sl = slice(hd * HEAD_DIM, (hd + 1) * HEAD_DIM)
        q_ref[:, sl] = rope(pq[:, sl]).astype(BF16)
    for hd in range(kv_w // HEAD_DIM):
        sl = slice(hd * HEAD_DIM, (hd + 1) * HEAD_DIM)
        kv_ref[:, sl] = rope(pkv[:, sl]).astype(BF16)
    kv_ref[:, kv_w:] = pkv[:, kv_w:].astype(BF16)


def _row_sources(x_lat, x_ctx, tm, n_lat_tiles):
    d = x_lat.shape[1]
    lat_map = lambda i: (jnp.minimum(i, n_lat_tiles - 1), 0)
    ctx_map = lambda i: (jnp.maximum(i - n_lat_tiles, 0), 0)
    return [pl.BlockSpec((tm, d), lat_map), pl.BlockSpec((tm, d), ctx_map)], [x_lat, x_ctx]


def _inproj(xs, norm_g, modr, cos, sin, w_in, l, *, n_rows, n_lat_rows, seq):
    d = xs[0].shape[1]
    in_w = w_in.shape[-1]
    q_w = d // 2
    kv_w = q_w // GROUP
    puz_w = in_w - q_w - 2 * kv_w
    tm = ROW_TILE
    tiles_per_seq = seq // tm
    n_lat_tiles = n_lat_rows // tm
    n_groups = n_lat_tiles // tiles_per_seq
    split_src = len(xs) == 2

    def mod_idx(k):
        return lambda i: (_mod_index(l, jnp.minimum(i // tiles_per_seq, n_groups), k), 0, 0)

    def rope_idx(i):
        return (jnp.where(i < n_lat_tiles, i % tiles_per_seq, tiles_per_seq), 0)

    if split_src:
        x_specs, x_args = _row_sources(xs[0], xs[1], tm, n_lat_tiles)
    else:
        x_specs, x_args = [pl.BlockSpec((tm, d), lambda i: (i, 0))], [xs[0]]

    kern = functools.partial(_inproj_kernel, n_lat_tiles=n_lat_tiles, q_w=q_w, kv_w=kv_w,
                             split_src=split_src)
    return pl.pallas_call(
        kern,
        grid=(n_rows // tm,),
        in_specs=x_specs + [
            pl.BlockSpec((None, 1, d), lambda i: (l, 0, 0)),
            pl.BlockSpec((None, 1, d), mod_idx(0)),
            pl.BlockSpec((None, 1, d), mod_idx(1)),
            pl.BlockSpec((tm, HEAD_DIM), rope_idx),
            pl.BlockSpec((tm, HEAD_DIM), rope_idx),
            pl.BlockSpec((d, in_w), lambda i: (0, 0), pipeline_mode=pl.Buffered(1)),
        ],
        out_specs=[
            pl.BlockSpec((tm, q_w), lambda i: (i, 0)),
            pl.BlockSpec((tm, 2 * kv_w), lambda i: (i, 0)),
            pl.BlockSpec((tm, puz_w), lambda i: (i, 0)),
        ],
        out_shape=[
            jax.ShapeDtypeStruct((n_rows, q_w), BF16),
            jax.ShapeDtypeStruct((n_rows, 2 * kv_w), BF16),
            jax.ShapeDtypeStruct((n_rows, puz_w), F32),
        ],
        scratch_shapes=[pltpu.VMEM((tm, d), BF16)],
        compiler_params=_cparams(("parallel",)),
        name="in_proj",
    )(*x_args, norm_g, modr, modr, cos, sin, w_in)


def _gelu(x):
    return 0.5 * x * (1.0 + lax.erf(x * (0.5 ** 0.5)))


def _mixer_kernel(sink_ref, q_ref, *refs, windowed, blocks_per_seq, n_kv_heads, pool_w, sgu_w):
    if windowed:
        kvp_ref, kvc_ref, kvn_ref = refs[:3]
        refs = refs[3:]
    kvx_ref, puz_ref, php_ref, phn_ref, pw_ref, ps_ref, sg_ref, sw_ref, sb_ref, o_ref, ext_ref = refs
    blk = WINDOW_BLOCK
    hd = HEAD_DIM
    n = pl.program_id(0) % blocks_per_seq
    nblk = blocks_per_seq
    has_prev = (n > 0).astype(jnp.int32)
    has_next = (n < nblk - 1).astype(jnp.int32)
    kv_w = n_kv_heads * hd
    q_w = kv_w * GROUP
    logit_scale = hd ** -0.5 * LOG2E

    def attend(keys, vals, tile_masks, kvh):
        nk = keys.shape[0]
        qs = jnp.concatenate(
            [q_ref[:, (kvh * GROUP + g) * hd:(kvh * GROUP + g + 1) * hd] for g in range(GROUP)],
            axis=0)
        s = lax.dot_general(qs, keys, (((1,), (1,)), ((), ())),
                            preferred_element_type=F32) * logit_scale
        vals1 = jnp.concatenate([vals, jnp.ones((nk, hd), BF16)], axis=1)
        es, sink_terms = [], []
        for g in range(GROUP):
            tiles = []
            for t, mask in enumerate(tile_masks):
                st = s[g * blk:(g + 1) * blk, t * blk:(t + 1) * blk]
                tiles.append(st if mask is None else jnp.where(mask, st, NEG))
            mt = tiles[0]
            for st in tiles[1:]:
                mt = jnp.maximum(mt, st)
            snk = sink_ref[kvh * GROUP + g] * LOG2E
            m = jnp.maximum(jnp.max(mt, axis=-1, keepdims=True), snk)
            es.append(jnp.concatenate([jnp.exp2(st - m).astype(BF16) for st in tiles], axis=1))
            sink_terms.append(jnp.exp2(snk - m))
        o = _dot(jnp.concatenate(es, axis=0), vals1)
        for g in range(GROUP):
            h = kvh * GROUP + g
            og = o[g * blk:(g + 1) * blk]
            o_ref[:, h * hd:(h + 1) * hd] = (og[:, :hd] / (og[:, hd:] + sink_terms[g])).astype(BF16)

    if windowed:
        qi = lax.broadcasted_iota(jnp.int32, (blk, blk), 0)
        kj = lax.broadcasted_iota(jnp.int32, (blk, blk), 1)
        mask_prev = kj >= qi + (1 - has_prev) * blk
        mask_next = kj <= qi - (1 - has_next) * blk
        masks = [mask_prev, None, mask_next] + [None] * (kvx_ref.shape[0] // blk)
        for kvh in range(n_kv_heads):
            ks = slice(kvh * hd, (kvh + 1) * hd)
            vs = slice(kv_w + kvh * hd, kv_w + (kvh + 1) * hd)
            keys = jnp.concatenate([kvp_ref[:, ks], kvc_ref[:, ks], kvn_ref[:, ks], kvx_ref[:, ks]], axis=0)
            vals = jnp.concatenate([kvp_ref[:, vs], kvc_ref[:, vs], kvn_ref[:, vs], kvx_ref[:, vs]], axis=0)
            attend(keys, vals, masks, kvh)
    else:
        masks = [None] * (kvx_ref.shape[0] // blk)
        for kvh in range(n_kv_heads):
            attend(kvx_ref[:, kvh * hd:(kvh + 1) * hd],
                   kvx_ref[:, kv_w + kvh * hd:kv_w + (kvh + 1) * hd], masks, kvh)

    halo = POOL_HALO
    keep_prev = jnp.full((halo, pool_w), has_prev, jnp.int32) > 0
    keep_next = jnp.full((halo, pool_w), has_next, jnp.int32) > 0
    ext_ref[0:halo, :] = jnp.where(keep_prev, php_ref[...], 0.0)
    ext_ref[halo + blk:, :] = jnp.where(keep_next, phn_ref[...], 0.0)
    ext_ref[halo:halo + blk, :] = puz_ref[:, :pool_w]
    pos = n * blk + lax.broadcasted_iota(jnp.int32, (blk, hd), 0)
    seq_len = nblk * blk
    for g, win in enumerate(POOL_WINDOWS):
        cs = slice(g * hd, (g + 1) * hd)
        half = win // 2
        acc = ext_ref[halo - half:halo - half + blk, cs]
        for d in range(-half + 1, half):
            acc = acc + ext_ref[halo + d:halo + d + blk, cs]
        cnt = (jnp.minimum(pos + half, seq_len) - jnp.maximum(pos - half, 0)).astype(F32)
        pooled = acc / cnt - puz_ref[:, cs]
        y = _dot(pooled.astype(BF16), pw_ref[g]) * ps_ref[:, cs]
        o_ref[:, q_w + g * hd:q_w + (g + 1) * hd] = y.astype(BF16)

    for g in range(sgu_w // hd):
        cs = slice(g * hd, (g + 1) * hd)
        u = _gelu(puz_ref[:, pool_w + g * hd:pool_w + (g + 1) * hd])
        z = _gelu(puz_ref[:, pool_w + sgu_w + g * hd:pool_w + sgu_w + (g + 1) * hd])
        mu = jnp.mean(z, axis=-1, keepdims=True)
        zc = z - mu
        var = jnp.mean(zc * zc, axis=-1, keepdims=True)
        zn = zc * lax.rsqrt(var + EPS) * sg_ref[:, cs]
        mixed = _dot(sw_ref[g], zn.astype(BF16)) + sb_ref[g]
        o_ref[:, q_w + pool_w + g * hd:q_w + pool_w + (g + 1) * hd] = (u * mixed).astype(BF16)


def _mixer(q, kv, puz, sink, pool_w_l, pool_scale_l, sgu_g_l, sgu_w_l, sgu_b_l, *,
           n_lat_rows, seq, ctx_len, windowed):
    blk = WINDOW_BLOCK
    q_w = q.shape[1]
    kv_w = kv.shape[1] // 2
    pool_w = pool_scale_l.shape[-1]
    sgu_w = sgu_g_l.shape[-1]
    d_mix = q_w + pool_w + sgu_w
    per_seq = (seq if windowed else ctx_len) // blk
    base = 0 if windowed else n_lat_rows // blk
    rows = n_lat_rows if windowed else q.shape[0] - n_lat_rows
    first_ctx_block = n_lat_rows // ctx_len
    hpb = blk // POOL_HALO

    def cur(i):
        return (base + i, 0)

    def prev_idx(i):
        return (base + jnp.where(i % per_seq > 0, i - 1, i), 0)

    def next_idx(i):
        return (base + jnp.where(i % per_seq < per_seq - 1, i + 1, i), 0)

    def ctx_idx(i):
        return (first_ctx_block + i // per_seq, 0)

    def halo_prev_idx(i):
        return (jnp.where(i % per_seq > 0, (base + i) * hpb - 1, (base + i) * hpb), 0)

    def halo_next_idx(i):
        return (jnp.where(i % per_seq < per_seq - 1, (base + i + 1) * hpb, (base + i) * hpb), 0)

    kern = functools.partial(
        _mixer_kernel, windowed=windowed, blocks_per_seq=per_seq,
        n_kv_heads=kv_w // HEAD_DIM, pool_w=pool_w, sgu_w=sgu_w)
    n_pool = pool_w // HEAD_DIM
    n_sgu = sgu_w // HEAD_DIM
    window_specs, window_args = [], []
    if windowed:
        window_specs = [pl.BlockSpec((blk, 2 * kv_w), prev_idx),
                        pl.BlockSpec((blk, 2 * kv_w), cur),
                        pl.BlockSpec((blk, 2 * kv_w), next_idx)]
        window_args = [kv, kv, kv]
    return pl.pallas_call(
        kern,
        grid=(rows // blk,),
        in_specs=[
            pl.BlockSpec(memory_space=pltpu.SMEM),
            pl.BlockSpec((blk, q_w), cur),
        ] + window_specs + [
            pl.BlockSpec((ctx_len, 2 * kv_w), ctx_idx),
            pl.BlockSpec((blk, pool_w + 2 * sgu_w), cur),
            pl.BlockSpec((POOL_HALO, pool_w), halo_prev_idx),
            pl.BlockSpec((POOL_HALO, pool_w), halo_next_idx),
            pl.BlockSpec((n_pool, HEAD_DIM, HEAD_DIM), lambda i: (0, 0, 0)),
            pl.BlockSpec((1, pool_w), lambda i: (0, 0)),
            pl.BlockSpec((1, sgu_w), lambda i: (0, 0)),
            pl.BlockSpec((n_sgu, blk, blk), lambda i: (0, 0, 0)),
            pl.BlockSpec((n_sgu, blk, HEAD_DIM), lambda i: (0, 0, 0)),
        ],
        out_specs=pl.BlockSpec((blk, d_mix), lambda i: (i, 0)),
        out_shape=jax.ShapeDtypeStruct((rows, d_mix), BF16),
        scratch_shapes=[pltpu.VMEM((blk + 2 * POOL_HALO, pool_w), F32)],
        compiler_params=_cparams(("parallel",)),
        name="mixer_lat" if windowed else "mixer_ctx",
    )(sink, q, *window_args, kv, puz, puz, puz, pool_w_l, pool_scale_l, sgu_g_l, sgu_w_l, sgu_b_l)


def _outproj_kernel(*refs, n_lat_tiles, split_mix, split_x):
    refs = list(refs)
    ma_ref = refs.pop(0)
    mb_ref = refs.pop(0) if split_mix else None
    xa_ref = refs.pop(0)
    xb_ref = refs.pop(0) if split_x else None
    gate_ref, w_ref, o_ref = refs
    is_lat = (pl.program_id(0) < n_lat_tiles).astype(jnp.int32)
    x = _pick_rows(xa_ref, xb_ref, slice(None), is_lat)
    mix = _pick_rows(ma_ref, mb_ref, slice(None), is_lat)
    o_ref[...] = x + gate_ref[...] * _dot(mix, w_ref[...])


def _outproj(mixes, xs, modr, w_out, l, *, seq, n_lat_rows):
    rows = sum(m.shape[0] for m in mixes)
    d_mix = mixes[0].shape[1]
    d = xs[0].shape[1]
    tm = ROW_TILE
    tiles_per_seq = seq // tm
    n_lat_tiles = n_lat_rows // tm
    n_groups = n_lat_rows // seq

    def sources(arrs):
        if len(arrs) == 2:
            return _row_sources(arrs[0], arrs[1], tm, n_lat_tiles)
        return [pl.BlockSpec((tm, arrs[0].shape[1]), lambda i: (i, 0))], [arrs[0]]

    mix_specs, mix_args = sources(mixes)
    x_specs, x_args = sources(xs)
    kern = functools.partial(_outproj_kernel, n_lat_tiles=n_lat_tiles,
                             split_mix=len(mixes) == 2, split_x=len(xs) == 2)
    return pl.pallas_call(
        kern,
        grid=(rows // tm,),
        in_specs=mix_specs + x_specs + [
            pl.BlockSpec((None, 1, d),
                         lambda i: (_mod_index(l, jnp.minimum(i // tiles_per_seq, n_groups), 2), 0, 0)),
            pl.BlockSpec((d_mix, d), lambda i: (0, 0), pipeline_mode=pl.Buffered(1)),
        ],
        out_specs=pl.BlockSpec((tm, d), lambda i: (i, 0)),
        out_shape=jax.ShapeDtypeStruct((rows, d), F32),
        compiler_params=_cparams(("parallel",)),
        name="out_proj",
    )(*mix_args, *x_args, modr, w_out)


def _ffn_kernel(*refs, final_norm, n_cast):
    (x_ref, g_ref, sh_ref, sc_ref, gate_ref, wg_ref, wu_ref, wd_ref, fg_ref), refs = refs[:9], refs[9:]
    cast_src, refs = refs[:n_cast], refs[n_cast:]
    o_ref, refs = refs[0], refs[1:]
    cast_dst, (xn_ref,) = refs[:n_cast], refs[n_cast:]
    f = pl.program_id(1)
    nf = pl.num_programs(1)
    tm, d = x_ref.shape
    n_chunks = tm // NORM_CHUNK

    @pl.when(f == 0)
    def _():
        gain = g_ref[...] * (1.0 + sc_ref[...])
        shift = sh_ref[...]

        def body(r, carry):
            rows = pl.ds(pl.multiple_of(r * NORM_CHUNK, NORM_CHUNK), NORM_CHUNK)
            h = _modulated_norm(x_ref[rows, :], gain, shift)
            xn_ref[rows, :] = h.astype(BF16)
            o_ref[rows, :] = jnp.zeros((NORM_CHUNK, d), F32)
            return carry
        lax.fori_loop(0, n_chunks, body, 0, unroll=NORM_UNROLL)

    for src, dst in zip(cast_src, cast_dst):
        dst[...] = src[...].astype(BF16)

    xn = xn_ref[...]
    gg = _dot(xn, wg_ref[...])
    uu = _dot(xn, wu_ref[...])
    act = (gg * jax.nn.sigmoid(gg) * uu).astype(BF16)
    for cidx in range(d // FFN_COL_TILE):
        cs = slice(cidx * FFN_COL_TILE, (cidx + 1) * FFN_COL_TILE)
        o_ref[:, cs] += _dot(act, wd_ref[:, cs])

    @pl.when(f == nf - 1)
    def _():
        def body(r, carry):
            rows = pl.ds(pl.multiple_of(r * NORM_CHUNK, NORM_CHUNK), NORM_CHUNK)
            y = x_ref[rows, :] + gate_ref[...] * o_ref[rows, :]
            if final_norm:
                ms = jnp.mean(y * y, axis=-1, keepdims=True)
                y = y * lax.rsqrt(ms + EPS) * fg_ref[...]
            o_ref[rows, :] = y
            return carry
        lax.fori_loop(0, n_chunks, body, 0, unroll=NORM_UNROLL)


def _cast_chunks(weights, l_next, n_i, n_f):
    def split(n, unit, limit):
        return max(k for k in range(1, limit + 1) if n % (k * unit) == 0)

    in_specs, out_specs, out_shapes = [], [], []
    for w in weights:
        _, r, c = w.shape
        rows_i = (split(r, BF16_SUBLANES, n_i), split(c, LANES, n_f))
        rows_f = (split(r, BF16_SUBLANES, n_f), split(c, LANES, n_i))
        if rows_i[0] * rows_i[1] >= rows_f[0] * rows_f[1]:
            kr, kc = rows_i
            idx = lambda i, f, kr=kr, kc=kc: (
                jnp.minimum(i, kr - 1), jnp.where(i < kr, jnp.minimum(f, kc - 1), kc - 1))
        else:
            kr, kc = rows_f
            idx = lambda i, f, kr=kr, kc=kc: (
                jnp.where(i < kc, jnp.minimum(f, kr - 1), kr - 1), jnp.minimum(i, kc - 1))
        blk = (r // kr, c // kc)
        in_specs.append(pl.BlockSpec((None,) + blk, lambda i, f, idx=idx: (l_next,) + idx(i, f)))
        out_specs.append(pl.BlockSpec(blk, idx))
        out_shapes.append(jax.ShapeDtypeStruct((r, c), BF16))
    return in_specs, out_specs, out_shapes


def _ffn(xt, norm_g, modr, w_gu, w_dn, final_g, l, *, rows, seq, n_lat_rows, final_norm, cast_next=()):
    d = xt.shape[1]
    d_ff = w_dn.shape[0]
    tm = FFN_ROW_TILE
    tf = FFN_COL_TILE
    nf = d_ff // tf
    tiles_per_seq = seq // tm
    n_groups = n_lat_rows // seq

    def mod_idx(k):
        return lambda i, f: (_mod_index(l, jnp.minimum(i // tiles_per_seq, n_groups), k), 0, 0)

    cast_in, cast_out, cast_shapes = _cast_chunks(cast_next, l + 1, rows // tm, nf)
    kern = functools.partial(_ffn_kernel, final_norm=final_norm, n_cast=len(cast_next))
    outs = pl.pallas_call(
        kern,
        grid=(rows // tm, nf),
        in_specs=[
            pl.BlockSpec((tm, d), lambda i, f: (i, 0)),
            pl.BlockSpec((None, 1, d), lambda i, f: (l, 0, 0)),
            pl.BlockSpec((None, 1, d), mod_idx(3)),
            pl.BlockSpec((None, 1, d), mod_idx(4)),
            pl.BlockSpec((None, 1, d), mod_idx(5)),
            pl.BlockSpec((d, tf), lambda i, f: (0, f)),
            pl.BlockSpec((d, tf), lambda i, f: (0, nf + f)),
            pl.BlockSpec((tf, d), lambda i, f: (f, 0)),
            pl.BlockSpec((1, d), lambda i, f: (0, 0)),
        ] + cast_in,
        out_specs=[pl.BlockSpec((tm, d), lambda i, f: (i, 0))] + cast_out,
        out_shape=[jax.ShapeDtypeStruct((rows, d), F32)] + cast_shapes,
        scratch_shapes=[pltpu.VMEM((tm, d), BF16)],
        compiler_params=_cparams(("arbitrary", "arbitrary"), FFN_VMEM_LIMIT),
        name="ffn",
    )(xt, norm_g, modr, modr, modr, w_gu, w_gu, w_dn, final_g, *cast_next)
    return outs[0], tuple(outs[1:])


def _rope_tables(seq, pad_rows):
    rows = seq // GRID_W
    inv_freq = ROPE_THETA ** (-jnp.arange(0, ROPE_AXIS, 2, dtype=F32) / ROPE_AXIS)
    row = jnp.repeat(jnp.arange(rows, dtype=F32), GRID_W)
    col = jnp.tile(jnp.arange(GRID_W, dtype=F32), rows)
    ar = row[:, None] * inv_freq
    ac = col[:, None] * inv_freq
    cos = jnp.concatenate([jnp.cos(ar), jnp.cos(ar), jnp.cos(ac), jnp.cos(ac)], axis=1)
    sin = jnp.concatenate([-jnp.sin(ar), jnp.sin(ar), -jnp.sin(ac), jnp.sin(ac)], axis=1)
    cos = jnp.concatenate([cos, jnp.ones((pad_rows, HEAD_DIM), F32)], axis=0)
    sin = jnp.concatenate([sin, jnp.zeros((pad_rows, HEAD_DIM), F32)], axis=0)
    return cos, sin


def kernel(x, c, ctx, c_ctx, w_ada, b_ada, norm_mix_g, norm_ffn_g, w_in, attn_sink, pool_w, pool_scale,
           sgu_norm_g, sgu_w, sgu_b, w_out, w_gate_up, w_down, final_norm_g):
    bsz, seq, d = x.shape
    ctx_len = ctx.shape[1]
    depth = w_in.shape[0]
    n_lat_rows = bsz * seq
    n_rows = n_lat_rows + bsz * ctx_len

    cond = jnp.concatenate([c, c_ctx[None, :], jnp.zeros((MOD_ROWS - bsz - 1, d), F32)], axis=0)
    mod = _ada_table(cond, w_ada, b_ada)
    modr = mod.reshape(depth * MOD_ROWS * MOD_CHUNKS, 1, d)

    cos, sin = _rope_tables(seq, ROW_TILE)
    big_weights = (w_in, w_out, w_gate_up, w_down)
    w_in_b, w_out_b, w_gu_b, w_dn_b = (w[0].astype(BF16) for w in big_weights)
    pool_w_b = pool_w.astype(BF16)
    sgu_w_b = sgu_w.astype(BF16)
    norm_mix = norm_mix_g.reshape(depth, 1, d)
    norm_ffn = norm_ffn_g.reshape(depth, 1, d)
    final_g = final_norm_g.reshape(1, d)

    xs = (x.reshape(n_lat_rows, d), ctx.reshape(bsz * ctx_len, d))
    for l in range(depth):
        last = l == depth - 1
        rows = n_lat_rows if last else n_rows
        q, kv, puz = _inproj(xs, norm_mix, modr, cos, sin, w_in_b, l,
                             n_rows=n_rows, n_lat_rows=n_lat_rows, seq=seq)
        sgu_b_full = jnp.broadcast_to(sgu_b[l][:, :, None], sgu_b.shape[1:] + (HEAD_DIM,))
        mixes = tuple(
            _mixer(q, kv, puz, attn_sink[l], pool_w_b[l], pool_scale[l].reshape(1, -1),
                   sgu_norm_g[l].reshape(1, -1), sgu_w_b[l], sgu_b_full,
                   n_lat_rows=n_lat_rows, seq=seq, ctx_len=ctx_len, windowed=windowed)
            for windowed in ((True,) if last else (True, False)))
        xt = _outproj(mixes, xs, modr, w_out_b, l, seq=seq, n_lat_rows=n_lat_rows)
        xt, next_w = _ffn(xt, norm_ffn, modr, w_gu_b, w_dn_b, final_g, l, rows=rows, seq=seq,
                          n_lat_rows=n_lat_rows, final_norm=last,
                          cast_next=() if last else big_weights)
        if not last:
            w_in_b, w_out_b, w_gu_b, w_dn_b = next_w
        xs = (xt,)
    return xt.reshape(bsz, seq, d)
```

```python
import functools

import jax
import jax.numpy as jnp
from jax import lax
from jax.experimental import pallas as pl
from jax.experimental.pallas import tpu as pltpu

F32 = jnp.float32
BF16 = jnp.bfloat16

GRID_W = 64
HEAD_DIM = 128
GROUP = 4
WINDOW_BLOCK = 128
ROPE_THETA = 10000.0
ROPE_AXIS = HEAD_DIM // 2
POOL_WINDOWS = (2, 4, 8, 16)
POOL_HALO = 8
EPS = 1e-6
NEG = -1e30
LOG2E = 1.4426950408889634

MOD_ROWS = 8
MOD_CHUNKS = 6

LANES = 128
BF16_SUBLANES = 16
VMEM_LIMIT = 56 * 1024 * 1024
FFN_VMEM_LIMIT = 62 * 1024 * 1024
ROW_TILE = 512
INPROJ_SUBTILES = 2
FFN_ROW_TILE = 1024
FFN_COL_TILE = 512
NORM_CHUNK = 32
NORM_UNROLL = 4


def _cparams(sem, vmem_limit=VMEM_LIMIT):
    return pltpu.CompilerParams(dimension_semantics=sem, vmem_limit_bytes=vmem_limit)


def _dot(a, b):
    return jnp.dot(a, b, preferred_element_type=F32)


def _mod_index(l, group, chunk):
    return (l * MOD_ROWS + group) * MOD_CHUNKS + chunk


def _ada_kernel(c_ref, w_ref, b_ref, o_ref):
    cv = c_ref[...]
    a = (cv * jax.nn.sigmoid(cv)).astype(BF16)
    o_ref[...] = _dot(a, w_ref[...].astype(BF16)) + b_ref[...]


def _ada_table(cond, w_ada, b_ada):
    depth, d, n = w_ada.shape
    tn = 1024
    return pl.pallas_call(
        _ada_kernel,
        grid=(depth, n // tn),
        in_specs=[
            pl.BlockSpec((MOD_ROWS, d), lambda l, j: (0, 0)),
            pl.BlockSpec((None, d, tn), lambda l, j: (l, 0, j)),
            pl.BlockSpec((None, 1, tn), lambda l, j: (l, 0, j)),
        ],
        out_specs=pl.BlockSpec((None, MOD_ROWS, tn), lambda l, j: (l, 0, j)),
        out_shape=jax.ShapeDtypeStruct((depth, MOD_ROWS, n), F32),
        compiler_params=_cparams(("parallel", "parallel")),
        name="ada_table",
    )(cond, w_ada, b_ada.reshape(depth, 1, n))


def _modulated_norm(x, gain, shift):
    ms = jnp.mean(x * x, axis=-1, keepdims=True)
    return x * lax.rsqrt(ms + EPS) * gain + shift


def _pick_rows(lat_ref, ctx_ref, rows, is_lat_tile):
    if ctx_ref is None:
        return lat_ref[rows, :]
    a = lat_ref[rows, :]
    sel = jnp.full(a.shape, is_lat_tile, jnp.int32) > 0
    return jnp.where(sel, a, ctx_ref[rows, :])


def _inproj_kernel(*refs, n_lat_tiles, q_w, kv_w, split_src):
    if split_src:
        xa_ref, xb_ref = refs[:2]
        refs = refs[2:]
    else:
        xa_ref, xb_ref = refs[0], None
        refs = refs[1:]
    g_ref, sh_ref, sc_ref, cos_ref, sin_ref, w_ref, q_ref, kv_ref, puz_ref, hb_ref = refs
    i = pl.program_id(0)
    tm = hb_ref.shape[0]
    is_lat = (i < n_lat_tiles).astype(jnp.int32)

    gain = g_ref[...] * (1.0 + sc_ref[...])
    shift = sh_ref[...]

    sub = tm // INPROJ_SUBTILES
    lane = lax.broadcasted_iota(jnp.int32, (sub, HEAD_DIM), 1)
    first_half = (lane % ROPE_AXIS) < (ROPE_AXIS // 2)

    for s in range(INPROJ_SUBTILES):
        for c in range(sub // NORM_CHUNK):
            rows = slice(s * sub + c * NORM_CHUNK, s * sub + (c + 1) * NORM_CHUNK)
            xs = _pick_rows(xa_ref, xb_ref, rows, is_lat)
            hb_ref[rows, :] = _modulated_norm(xs, gain, shift).astype(BF16)
        rs = slice(s * sub, (s + 1) * sub)
        hb = hb_ref[rs, :]
        pq = _dot(hb, w_ref[:, :q_w])
        pkv = _dot(hb, w_ref[:, q_w:q_w + 2 * kv_w])
        puz_ref[rs, :] = _dot(hb, w_ref[:, q_w + 2 * kv_w:])

        cos = cos_ref[rs, :]
        sin = sin_ref[rs, :]

        def rope(t):
            rot = jnp.where(first_half,
                            pltpu.roll(t, HEAD_DIM - ROPE_AXIS // 2, 1),
                            pltpu.roll(t, ROPE_AXIS // 2, 1))
            return t * cos + rot * sin

        for hd in range(q_w // HEAD_DIM):
            sl = slice(hd * HEAD_DIM, (hd + 1) * HEAD_DIM)
            q_ref[rs, sl] = rope(pq[:, sl]).astype(BF16)
        for hd in range(kv_w // HEAD_DIM):
            sl = slice(hd * HEAD_DIM, (hd + 1) * HEAD_DIM)
            kv_ref[rs, sl] = rope(pkv[:, sl]).astype(BF16)
        kv_ref[rs, kv_w:] = pkv[:, kv_w:].astype(BF16)


def _row_sources(x_lat, x_ctx, tm, n_lat_tiles):
    d = x_lat.shape[1]
    lat_map = lambda i: (jnp.minimum(i, n_lat_tiles - 1), 0)
    ctx_map = lambda i: (jnp.maximum(i - n_lat_tiles, 0), 0)
    return [pl.BlockSpec((tm, d), lat_map), pl.BlockSpec((tm, d), ctx_map)], [x_lat, x_ctx]


def _inproj(xs, norm_g, modr, cos, sin, w_in, l, *, n_rows, n_lat_rows, seq):
    d = xs[0].shape[1]
    in_w = w_in.shape[-1]
    q_w = d // 2
    kv_w = q_w // GROUP
    puz_w = in_w - q_w - 2 * kv_w
    tm = ROW_TILE
    tiles_per_seq = seq // tm
    n_lat_tiles = n_lat_rows // tm
    n_groups = n_lat_tiles // tiles_per_seq
    split_src = len(xs) == 2

    def mod_idx(k):
        return lambda i: (_mod_index(l, jnp.minimum(i // tiles_per_seq, n_groups), k), 0, 0)

    def rope_idx(i):
        return (jnp.where(i < n_lat_tiles, i % tiles_per_seq, tiles_per_seq), 0)

    if split_src:
        x_specs, x_args = _row_sources(xs[0], xs[1], tm, n_lat_tiles)
    else:
        x_specs, x_args = [pl.BlockSpec((tm, d), lambda i: (i, 0))], [xs[0]]

    kern = functools.partial(_inproj_kernel, n_lat_tiles=n_lat_tiles, q_w=q_w, kv_w=kv_w,
                             split_src=split_src)
    return pl.pallas_call(
        kern,
        grid=(n_rows // tm,),
        in_specs=x_specs + [
            pl.BlockSpec((None, 1, d), lambda i: (l, 0, 0)),
            pl.BlockSpec((None, 1, d), mod_idx(0)),
            pl.BlockSpec((None, 1, d), mod_idx(1)),
            pl.BlockSpec((tm, HEAD_DIM), rope_idx),
            pl.BlockSpec((tm, HEAD_DIM), rope_idx),
            pl.BlockSpec((d, in_w), lambda i: (0, 0), pipeline_mode=pl.Buffered(1)),
        ],
        out_specs=[
            pl.BlockSpec((tm, q_w), lambda i: (i, 0)),
            pl.BlockSpec((tm, 2 * kv_w), lambda i: (i, 0)),
            pl.BlockSpec((tm, puz_w), lambda i: (i, 0)),
        ],
        out_shape=[
            jax.ShapeDtypeStruct((n_rows, q_w), BF16),
            jax.ShapeDtypeStruct((n_rows, 2 * kv_w), BF16),
            jax.ShapeDtypeStruct((n_rows, puz_w), F32),
        ],
        scratch_shapes=[pltpu.VMEM((tm, d), BF16)],
        compiler_params=_cparams(("parallel",)),
        name="in_proj",
    )(*x_args, norm_g, modr, modr, cos, sin, w_in)


def _gelu(x):
    return 0.5 * x * (1.0 + lax.erf(x * (0.5 ** 0.5)))


def _mixer_kernel(sink_ref, q_ref, *refs, windowed, blocks_per_seq, n_kv_heads, pool_w, sgu_w):
    if windowed:
        kvp_ref, kvc_ref, kvn_ref = refs[:3]
        refs = refs[3:]
    kvx_ref, puz_ref, php_ref, phn_ref, pw_ref, ps_ref, sg_ref, sw_ref, sb_ref, o_ref, ext_ref = refs
    blk = WINDOW_BLOCK
    hd = HEAD_DIM
    n = pl.program_id(0) % blocks_per_seq
    nblk = blocks_per_seq
    has_prev = (n > 0).astype(jnp.int32)
    has_next = (n < nblk - 1).astype(jnp.int32)
    kv_w = n_kv_heads * hd
    q_w = kv_w * GROUP
    logit_scale = hd ** -0.5 * LOG2E

    def attend(keys, vals, tile_masks, kvh):
        nk = keys.shape[0]
        qs = jnp.concatenate(
            [q_ref[:, (kvh * GROUP + g) * hd:(kvh * GROUP + g + 1) * hd] for g in range(GROUP)],
            axis=0)
        s = lax.dot_general(qs, keys, (((1,), (1,)), ((), ())),
                            preferred_element_type=F32) * logit_scale
        vals1 = jnp.concatenate([vals, jnp.ones((nk, hd), BF16)], axis=1)
        es, sink_terms = [], []
        for g in range(GROUP):
            tiles = []
            for t, mask in enumerate(tile_masks):
                st = s[g * blk:(g + 1) * blk, t * blk:(t + 1) * blk]
                tiles.append(st if mask is None else jnp.where(mask, st, NEG))
            mt = tiles[0]
            for st in tiles[1:]:
                mt = jnp.maximum(mt, st)
            snk = sink_ref[kvh * GROUP + g] * LOG2E
            m = jnp.maximum(jnp.max(mt, axis=-1, keepdims=True), snk)
            es.append(jnp.concatenate([jnp.exp2(st - m).astype(BF16) for st in tiles], axis=1))
            sink_terms.append(jnp.exp2(snk - m))
        o = _dot(jnp.concatenate(es, axis=0), vals1)
        for g in range(GROUP):
            h = kvh * GROUP + g
            og = o[g * blk:(g + 1) * blk]
            o_ref[:, h * hd:(h + 1) * hd] = (og[:, :hd] / (og[:, hd:] + sink_terms[g])).astype(BF16)

    if windowed:
        qi = lax.broadcasted_iota(jnp.int32, (blk, blk), 0)
        kj = lax.broadcasted_iota(jnp.int32, (blk, blk), 1)
        mask_prev = kj >= qi + (1 - has_prev) * blk
        mask_next = kj <= qi - (1 - has_next) * blk
        masks = [mask_prev, None, mask_next] + [None] * (kvx_ref.shape[0] // blk)
        for kvh in range(n_kv_heads):
            ks = slice(kvh * hd, (kvh + 1) * hd)
            vs = slice(kv_w + kvh * hd, kv_w + (kvh + 1) * hd)
            keys = jnp.concatenate([kvp_ref[:, ks], kvc_ref[:, ks], kvn_ref[:, ks], kvx_ref[:, ks]], axis=0)
            vals = jnp.concatenate([kvp_ref[:, vs], kvc_ref[:, vs], kvn_ref[:, vs], kvx_ref[:, vs]], axis=0)
            attend(keys, vals, masks, kvh)
    else:
        masks = [None] * (kvx_ref.shape[0] // blk)
        for kvh in range(n_kv_heads):
            attend(kvx_ref[:, kvh * hd:(kvh + 1) * hd],
                   kvx_ref[:, kv_w + kvh * hd:kv_w + (kvh + 1) * hd], masks, kvh)

    halo = POOL_HALO
    keep_prev = jnp.full((halo, pool_w), has_prev, jnp.int32) > 0
    keep_next = jnp.full((halo, pool_w), has_next, jnp.int32) > 0
    ext_ref[0:halo, :] = jnp.where(keep_prev, php_ref[...], 0.0)
    ext_ref[halo + blk:, :] = jnp.where(keep_next, phn_ref[...], 0.0)
    ext_ref[halo:halo + blk, :] = puz_ref[:, :pool_w]
    pos = n * blk + lax.broadcasted_iota(jnp.int32, (blk, hd), 0)
    seq_len = nblk * blk
    for g, win in enumerate(POOL_WINDOWS):
        cs = slice(g * hd, (g + 1) * hd)
        half = win // 2
        acc = ext_ref[halo - half:halo - half + blk, cs]
        for d in range(-half + 1, half):
            acc = acc + ext_ref[halo + d:halo + d + blk, cs]
        cnt = (jnp.minimum(pos + half, seq_len) - jnp.maximum(pos - half, 0)).astype(F32)
        pooled = acc / cnt - puz_ref[:, cs]
        y = _dot(pooled.astype(BF16), pw_ref[g]) * ps_ref[:, cs]
        o_ref[:, q_w + g * hd:q_w + (g + 1) * hd] = y.astype(BF16)

    for g in range(sgu_w // hd):
        cs = slice(g * hd, (g + 1) * hd)
        u = _gelu(puz_ref[:, pool_w + g * hd:pool_w + (g + 1) * hd])
        z = _gelu(puz_ref[:, pool_w + sgu_w + g * hd:pool_w + sgu_w + (g + 1) * hd])
        mu = jnp.mean(z, axis=-1, keepdims=True)
        zc = z - mu
        var = jnp.mean(zc * zc, axis=-1, keepdims=True)
        zn = zc * lax.rsqrt(var + EPS) * sg_ref[:, cs]
        mixed = _dot(sw_ref[g], zn.astype(BF16)) + sb_ref[g]
        o_ref[:, q_w + pool_w + g * hd:q_w + pool_w + (g + 1) * hd] = (u * mixed).astype(BF16)


def _mixer(q, kv, puz, sink, pool_w_l, pool_scale_l, sgu_g_l, sgu_w_l, sgu_b_l, *,
           n_lat_rows, seq, ctx_len, windowed):
    blk = WINDOW_BLOCK
    q_w = q.shape[1]
    kv_w = kv.shape[1] // 2
    pool_w = pool_scale_l.shape[-1]
    sgu_w = sgu_g_l.shape[-1]
    d_mix = q_w + pool_w + sgu_w
    per_seq = (seq if windowed else ctx_len) // blk
    base = 0 if windowed else n_lat_rows // blk
    rows = n_lat_rows if windowed else q.shape[0] - n_lat_rows
    first_ctx_block = n_lat_rows // ctx_len
    hpb = blk // POOL_HALO

    def cur(i):
        return (base + i, 0)

    def prev_idx(i):
        return (base + jnp.where(i % per_seq > 0, i - 1, i), 0)

    def next_idx(i):
        return (base + jnp.where(i % per_seq < per_seq - 1, i + 1, i), 0)

    def ctx_idx(i):
        return (first_ctx_block + i // per_seq, 0)

    def halo_prev_idx(i):
        return (jnp.where(i % per_seq > 0, (base + i) * hpb - 1, (base + i) * hpb), 0)

    def halo_next_idx(i):
        return (jnp.where(i % per_seq < per_seq - 1, (base + i + 1) * hpb, (base + i) * hpb), 0)

    kern = functools.partial(
        _mixer_kernel, windowed=windowed, blocks_per_seq=per_seq,
        n_kv_heads=kv_w // HEAD_DIM, pool_w=pool_w, sgu_w=sgu_w)
    n_pool = pool_w // HEAD_DIM
    n_sgu = sgu_w // HEAD_DIM
    window_specs, window_args = [], []
    if windowed:
        window_specs = [pl.BlockSpec((blk, 2 * kv_w), prev_idx),
                        pl.BlockSpec((blk, 2 * kv_w), cur),
                        pl.BlockSpec((blk, 2 * kv_w), next_idx)]
        window_args = [kv, kv, kv]
    return pl.pallas_call(
        kern,
        grid=(rows // blk,),
        in_specs=[
            pl.BlockSpec(memory_space=pltpu.SMEM),
            pl.BlockSpec((blk, q_w), cur),
        ] + window_specs + [
            pl.BlockSpec((ctx_len, 2 * kv_w), ctx_idx),
            pl.BlockSpec((blk, pool_w + 2 * sgu_w), cur),
            pl.BlockSpec((POOL_HALO, pool_w), halo_prev_idx),
            pl.BlockSpec((POOL_HALO, pool_w), halo_next_idx),
            pl.BlockSpec((n_pool, HEAD_DIM, HEAD_DIM), lambda i: (0, 0, 0)),
            pl.BlockSpec((1, pool_w), lambda i: (0, 0)),
            pl.BlockSpec((1, sgu_w), lambda i: (0, 0)),
            pl.BlockSpec((n_sgu, blk, blk), lambda i: (0, 0, 0)),
            pl.BlockSpec((n_sgu, blk, HEAD_DIM), lambda i: (0, 0, 0)),
        ],
        out_specs=pl.BlockSpec((blk, d_mix), lambda i: (i, 0)),
        out_shape=jax.ShapeDtypeStruct((rows, d_mix), BF16),
        scratch_shapes=[pltpu.VMEM((blk + 2 * POOL_HALO, pool_w), F32)],
        compiler_params=_cparams(("parallel",)),
        name="mixer_lat" if windowed else "mixer_ctx",
    )(sink, q, *window_args, kv, puz, puz, puz, pool_w_l, pool_scale_l, sgu_g_l, sgu_w_l, sgu_b_l)


def _outproj_kernel(*refs, n_lat_tiles, split_mix, split_x):
    refs = list(refs)
    ma_ref = refs.pop(0)
    mb_ref = refs.pop(0) if split_mix else None
    xa_ref = refs.pop(0)
    xb_ref = refs.pop(0) if split_x else None
    gate_ref, w_ref, o_ref = refs
    is_lat = (pl.program_id(0) < n_lat_tiles).astype(jnp.int32)
    x = _pick_rows(xa_ref, xb_ref, slice(None), is_lat)
    mix = _pick_rows(ma_ref, mb_ref, slice(None), is_lat)
    o_ref[...] = x + gate_ref[...] * _dot(mix, w_ref[...])


def _outproj(mixes, xs, modr, w_out, l, *, seq, n_lat_rows):
    rows = sum(m.shape[0] for m in mixes)
    d_mix = mixes[0].shape[1]
    d = xs[0].shape[1]
    tm = ROW_TILE
    tiles_per_seq = seq // tm
    n_lat_tiles = n_lat_rows // tm
    n_groups = n_lat_rows // seq

    def sources(arrs):
        if len(arrs) == 2:
            return _row_sources(arrs[0], arrs[1], tm, n_lat_tiles)
        return [pl.BlockSpec((tm, arrs[0].shape[1]), lambda i: (i, 0))], [arrs[0]]

    mix_specs, mix_args = sources(mixes)
    x_specs, x_args = sources(xs)
    kern = functools.partial(_outproj_kernel, n_lat_tiles=n_lat_tiles,
                             split_mix=len(mixes) == 2, split_x=len(xs) == 2)
    return pl.pallas_call(
        kern,
        grid=(rows // tm,),
        in_specs=mix_specs + x_specs + [
            pl.BlockSpec((None, 1, d),
                         lambda i: (_mod_index(l, jnp.minimum(i // tiles_per_seq, n_groups), 2), 0, 0)),
            pl.BlockSpec((d_mix, d), lambda i: (0, 0), pipeline_mode=pl.Buffered(1)),
        ],
        out_specs=pl.BlockSpec((tm, d), lambda i: (i, 0)),
        out_shape=jax.ShapeDtypeStruct((rows, d), F32),
        compiler_params=_cparams(("parallel",)),
        name="out_proj",
    )(*mix_args, *x_args, modr, w_out)


def _ffn_kernel(*refs, final_norm, n_cast):
    (x_ref, g_ref, sh_ref, sc_ref, gate_ref, wg_ref, wu_ref, wd_ref, fg_ref), refs = refs[:9], refs[9:]
    cast_src, refs = refs[:n_cast], refs[n_cast:]
    o_ref, refs = refs[0], refs[1:]
    cast_dst, (xn_ref,) = refs[:n_cast], refs[n_cast:]
    f = pl.program_id(1)
    nf = pl.num_programs(1)
    tm, d = x_ref.shape
    n_chunks = tm // NORM_CHUNK

    @pl.when(f == 0)
    def _():
        gain = g_ref[...] * (1.0 + sc_ref[...])
        shift = sh_ref[...]

        def body(r, carry):
            rows = pl.ds(pl.multiple_of(r * NORM_CHUNK, NORM_CHUNK), NORM_CHUNK)
            h = _modulated_norm(x_ref[rows, :], gain, shift)
            xn_ref[rows, :] = h.astype(BF16)
            o_ref[rows, :] = jnp.zeros((NORM_CHUNK, d), F32)
            return carry
        lax.fori_loop(0, n_chunks, body, 0, unroll=NORM_UNROLL)

    xn = xn_ref[...]
    gg = _dot(xn, wg_ref[...])
    uu = _dot(xn, wu_ref[...])
    act = (gg * jax.nn.sigmoid(gg) * uu).astype(BF16)
    casts = list(zip(cast_src, cast_dst))
    n_down = d // FFN_COL_TILE
    for cidx in range(n_down):
        for src, dst in casts[cidx::n_down]:
            dst[...] = src[...].astype(BF16)
        cs = slice(cidx * FFN_COL_TILE, (cidx + 1) * FFN_COL_TILE)
        o_ref[:, cs] += _dot(act, wd_ref[:, cs])

    @pl.when(f == nf - 1)
    def _():
        def body(r, carry):
            rows = pl.ds(pl.multiple_of(r * NORM_CHUNK, NORM_CHUNK), NORM_CHUNK)
            y = x_ref[rows, :] + gate_ref[...] * o_ref[rows, :]
            if final_norm:
                ms = jnp.mean(y * y, axis=-1, keepdims=True)
                y = y * lax.rsqrt(ms + EPS) * fg_ref[...]
            o_ref[rows, :] = y
            return carry
        lax.fori_loop(0, n_chunks, body, 0, unroll=NORM_UNROLL)


def _cast_chunks(weights, l_next, n_i, n_f):
    def split(n, unit, limit):
        return max(k for k in range(1, limit + 1) if n % (k * unit) == 0)

    in_specs, out_specs, out_shapes = [], [], []
    for w in weights:
        _, r, c = w.shape
        rows_i = (split(r, BF16_SUBLANES, n_i), split(c, LANES, n_f))
        rows_f = (split(r, BF16_SUBLANES, n_f), split(c, LANES, n_i))
        if rows_i[0] * rows_i[1] >= rows_f[0] * rows_f[1]:
            kr, kc = rows_i
            idx = lambda i, f, kr=kr, kc=kc: (
                jnp.minimum(i, kr - 1), jnp.where(i < kr, jnp.minimum(f, kc - 1), kc - 1))
        else:
            kr, kc = rows_f
            idx = lambda i, f, kr=kr, kc=kc: (
                jnp.where(i < kc, jnp.minimum(f, kr - 1), kr - 1), jnp.minimum(i, kc - 1))
        blk = (r // kr, c // kc)
        in_specs.append(pl.BlockSpec((None,) + blk, lambda i, f, idx=idx: (l_next,) + idx(i, f)))
        out_specs.append(pl.BlockSpec(blk, idx))
        out_shapes.append(jax.ShapeDtypeStruct((r, c), BF16))
    return in_specs, out_specs, out_shapes


def _ffn(xt, norm_g, modr, w_gu, w_dn, final_g, l, *, rows, seq, n_lat_rows, final_norm, cast_next=()):
    d = xt.shape[1]
    d_ff = w_dn.shape[0]
    tm = FFN_ROW_TILE
    tf = FFN_COL_TILE
    nf = d_ff // tf
    tiles_per_seq = seq // tm
    n_groups = n_lat_rows // seq

    def mod_idx(k):
        return lambda i, f: (_mod_index(l, jnp.minimum(i // tiles_per_seq, n_groups), k), 0, 0)

    cast_in, cast_out, cast_shapes = _cast_chunks(cast_next, l + 1, rows // tm, nf)
    kern = functools.partial(_ffn_kernel, final_norm=final_norm, n_cast=len(cast_next))
    outs = pl.pallas_call(
        kern,
        grid=(rows // tm, nf),
        in_specs=[
            pl.BlockSpec((tm, d), lambda i, f: (i, 0)),
            pl.BlockSpec((None, 1, d), lambda i, f: (l, 0, 0)),
            pl.BlockSpec((None, 1, d), mod_idx(3)),
            pl.BlockSpec((None, 1, d), mod_idx(4)),
            pl.BlockSpec((None, 1, d), mod_idx(5)),
            pl.BlockSpec((d, tf), lambda i, f: (0, f)),
            pl.BlockSpec((d, tf), lambda i, f: (0, nf + f)),
            pl.BlockSpec((tf, d), lambda i, f: (f, 0)),
            pl.BlockSpec((1, d), lambda i, f: (0, 0)),
        ] + cast_in,
        out_specs=[pl.BlockSpec((tm, d), lambda i, f: (i, 0))] + cast_out,
        out_shape=[jax.ShapeDtypeStruct((rows, d), F32)] + cast_shapes,
        scratch_shapes=[pltpu.VMEM((tm, d), BF16)],
        compiler_params=_cparams(("arbitrary", "arbitrary"), FFN_VMEM_LIMIT),
        name="ffn",
    )(xt, norm_g, modr, modr, modr, w_gu, w_gu, w_dn, final_g, *cast_next)
    return outs[0], tuple(outs[1:])


def _rope_tables(seq, pad_rows):
    rows = seq // GRID_W
    inv_freq = ROPE_THETA ** (-jnp.arange(0, ROPE_AXIS, 2, dtype=F32) / ROPE_AXIS)
    row = jnp.repeat(jnp.arange(rows, dtype=F32), GRID_W)
    col = jnp.tile(jnp.arange(GRID_W, dtype=F32), rows)
    ar = row[:, None] * inv_freq
    ac = col[:, None] * inv_freq
    cos = jnp.concatenate([jnp.cos(ar), jnp.cos(ar), jnp.cos(ac), jnp.cos(ac)], axis=1)
    sin = jnp.concatenate([-jnp.sin(ar), jnp.sin(ar), -jnp.sin(ac), jnp.sin(ac)], axis=1)
    cos = jnp.concatenate([cos, jnp.ones((pad_rows, HEAD_DIM), F32)], axis=0)
    sin = jnp.concatenate([sin, jnp.zeros((pad_rows, HEAD_DIM), F32)], axis=0)
    return cos, sin


def kernel(x, c, ctx, c_ctx, w_ada, b_ada, norm_mix_g, norm_ffn_g, w_in, attn_sink, pool_w, pool_scale,
           sgu_norm_g, sgu_w, sgu_b, w_out, w_gate_up, w_down, final_norm_g):
    bsz, seq, d = x.shape
    ctx_len = ctx.shape[1]
    depth = w_in.shape[0]
    n_lat_rows = bsz * seq
    n_rows = n_lat_rows + bsz * ctx_len

    cond = jnp.concatenate([c, c_ctx[None, :], jnp.zeros((MOD_ROWS - bsz - 1, d), F32)], axis=0)
    mod = _ada_table(cond, w_ada, b_ada)
    modr = mod.reshape(depth * MOD_ROWS * MOD_CHUNKS, 1, d)

    cos, sin = _rope_tables(seq, ROW_TILE)
    big_weights = (w_in, w_out, w_gate_up, w_down)
    w_in_b, w_out_b, w_gu_b, w_dn_b = (w[0].astype(BF16) for w in big_weights)
    pool_w_b = pool_w.astype(BF16)
    sgu_w_b = sgu_w.astype(BF16)
    norm_mix = norm_mix_g.reshape(depth, 1, d)
    norm_ffn = norm_ffn_g.reshape(depth, 1, d)
    final_g = final_norm_g.reshape(1, d)

    xs = (x.reshape(n_lat_rows, d), ctx.reshape(bsz * ctx_len, d))
    for l in range(depth):
        last = l == depth - 1
        rows = n_lat_rows if last else n_rows
        q, kv, puz = _inproj(xs, norm_mix, modr, cos, sin, w_in_b, l,
                             n_rows=n_rows, n_lat_rows=n_lat_rows, seq=seq)
        sgu_b_full = jnp.broadcast_to(sgu_b[l][:, :, None], sgu_b.shape[1:] + (HEAD_DIM,))
        mixes = tuple(
            _mixer(q, kv, puz, attn_sink[l], pool_w_b[l], pool_scale[l].reshape(1, -1),
                   sgu_norm_g[l].reshape(1, -1), sgu_w_b[l], sgu_b_full,
                   n_lat_rows=n_lat_rows, seq=seq, ctx_len=ctx_len, windowed=windowed)
            for windowed in ((True,) if last else (True, False)))
        xt = _outproj(mixes, xs, modr, w_out_b, l, seq=seq, n_lat_rows=n_lat_rows)
        xt, next_w = _ffn(xt, norm_ffn, modr, w_gu_b, w_dn_b, final_g, l, rows=rows, seq=seq,
                          n_lat_rows=n_lat_rows, final_norm=last,
                          cast_next=() if last else big_weights)
        if not last:
            w_in_b, w_out_b, w_gu_b, w_dn_b = next_w
        xs = (xt,)
    return xt.reshape(bsz, seq, d)
```

```python
import functools

import jax
import jax.numpy as jnp
from jax import lax
from jax.experimental import pallas as pl
from jax.experimental.pallas import tpu as pltpu

F32 = jnp.float32
BF16 = jnp.bfloat16

GRID_W = 64
HEAD_DIM = 128
GROUP = 4
WINDOW_BLOCK = 128
ROPE_THETA = 10000.0
ROPE_AXIS = HEAD_DIM // 2
POOL_WINDOWS = (2, 4, 8, 16)
POOL_HALO = 8
EPS = 1e-6
NEG = -1e30
LOG2E = 1.4426950408889634

MOD_ROWS = 8
MOD_CHUNKS = 6

LANES = 128
BF16_SUBLANES = 16
VMEM_LIMIT = 56 * 1024 * 1024
FFN_VMEM_LIMIT = 62 * 1024 * 1024
ROW_TILE = 512
INPROJ_SUBTILES = 2
FFN_ROW_TILE = 1024
FFN_COL_TILE = 512
NORM_CHUNK = 32
NORM_UNROLL = 4


def _cparams(sem, vmem_limit=VMEM_LIMIT):
    return pltpu.CompilerParams(dimension_semantics=sem, vmem_limit_bytes=vmem_limit)


def _dot(a, b):
    return jnp.dot(a, b, preferred_element_type=F32)


def _mod_index(group, chunk):
    return group * MOD_CHUNKS + chunk


def _ada_chunk(c_ref, w_ref, b_ref):
    cv = c_ref[...]
    a = (cv * jax.nn.sigmoid(cv)).astype(BF16)
    return _dot(a, w_ref[...].astype(BF16)) + b_ref[...]


def _ada_kernel(c_ref, w_ref, b_ref, o_ref):
    o_ref[...] = _ada_chunk(c_ref, w_ref, b_ref)


def _ada_table(cond, w_ada, b_ada3, l):
    _, d, n = w_ada.shape
    tn = 1024
    return pl.pallas_call(
        _ada_kernel,
        grid=(n // tn,),
        in_specs=[
            pl.BlockSpec((MOD_ROWS, d), lambda j: (0, 0)),
            pl.BlockSpec((None, d, tn), lambda j: (l, 0, j)),
            pl.BlockSpec((None, 1, tn), lambda j: (l, 0, j)),
        ],
        out_specs=pl.BlockSpec((MOD_ROWS, tn), lambda j: (0, j)),
        out_shape=jax.ShapeDtypeStruct((MOD_ROWS, n), F32),
        compiler_params=_cparams(("parallel",)),
        name="ada_table",
    )(cond, w_ada, b_ada3)


def _modulated_norm(x, gain, shift):
    ms = jnp.mean(x * x, axis=-1, keepdims=True)
    return x * lax.rsqrt(ms + EPS) * gain + shift


def _pick_rows(lat_ref, ctx_ref, rows, is_lat_tile):
    if ctx_ref is None:
        return lat_ref[rows, :]
    a = lat_ref[rows, :]
    sel = jnp.full(a.shape, is_lat_tile, jnp.int32) > 0
    return jnp.where(sel, a, ctx_ref[rows, :])


def _inproj_kernel(*refs, n_lat_tiles, q_w, kv_w, split_src, next_ada):
    refs = list(refs)
    xa_ref = refs.pop(0)
    xb_ref = refs.pop(0) if split_src else None
    g_ref, sh_ref, sc_ref, cos_ref, sin_ref, w_ref = refs[:6]
    refs = refs[6:]
    if next_ada:
        cond_ref, wada_ref, bada_ref = refs[:3]
        q_ref, kv_ref, puz_ref, modn_ref, hb_ref = refs[3:]
    else:
        q_ref, kv_ref, puz_ref, hb_ref = refs
    i = pl.program_id(0)
    tm = hb_ref.shape[0]
    is_lat = (i < n_lat_tiles).astype(jnp.int32)

    gain = g_ref[...] * (1.0 + sc_ref[...])
    shift = sh_ref[...]

    sub = tm // INPROJ_SUBTILES
    lane = lax.broadcasted_iota(jnp.int32, (sub, HEAD_DIM), 1)
    first_half = (lane % ROPE_AXIS) < (ROPE_AXIS // 2)

    for s in range(INPROJ_SUBTILES):
        for c in range(sub // NORM_CHUNK):
            rows = slice(s * sub + c * NORM_CHUNK, s * sub + (c + 1) * NORM_CHUNK)
            xs = _pick_rows(xa_ref, xb_ref, rows, is_lat)
            hb_ref[rows, :] = _modulated_norm(xs, gain, shift).astype(BF16)
        rs = slice(s * sub, (s + 1) * sub)
        if next_ada and s == 0:
            modn_ref[...] = _ada_chunk(cond_ref, wada_ref, bada_ref)
        hb = hb_ref[rs, :]
        pq = _dot(hb, w_ref[:, :q_w])
        pkv = _dot(hb, w_ref[:, q_w:q_w + 2 * kv_w])
        puz_ref[rs, :] = _dot(hb, w_ref[:, q_w + 2 * kv_w:])

        cos = cos_ref[rs, :]
        sin = sin_ref[rs, :]

        def rope(t):
            rot = jnp.where(first_half,
                            pltpu.roll(t, HEAD_DIM - ROPE_AXIS // 2, 1),
                            pltpu.roll(t, ROPE_AXIS // 2, 1))
            return t * cos + rot * sin

        for hd in range(q_w // HEAD_DIM):
            sl = slice(hd * HEAD_DIM, (hd + 1) * HEAD_DIM)
            q_ref[rs, sl] = rope(pq[:, sl]).astype(BF16)
        for hd in range(kv_w // HEAD_DIM):
            sl = slice(hd * HEAD_DIM, (hd + 1) * HEAD_DIM)
            kv_ref[rs, sl] = rope(pkv[:, sl]).astype(BF16)
        kv_ref[rs, kv_w:] = pkv[:, kv_w:].astype(BF16)


def _row_sources(x_lat, x_ctx, tm, n_lat_tiles):
    d = x_lat.shape[1]
    lat_map = lambda i: (jnp.minimum(i, n_lat_tiles - 1), 0)
    ctx_map = lambda i: (jnp.maximum(i - n_lat_tiles, 0), 0)
    return [pl.BlockSpec((tm, d), lat_map), pl.BlockSpec((tm, d), ctx_map)], [x_lat, x_ctx]


def _inproj(xs, norm_g, modr, cos, sin, w_in, l, *, n_rows, n_lat_rows, seq, ada_next=None):
    d = xs[0].shape[1]
    in_w = w_in.shape[-1]
    q_w = d // 2
    kv_w = q_w // GROUP
    puz_w = in_w - q_w - 2 * kv_w
    tm = ROW_TILE
    tiles_per_seq = seq // tm
    n_lat_tiles = n_lat_rows // tm
    n_groups = n_lat_tiles // tiles_per_seq
    split_src = len(xs) == 2

    def mod_idx(k):
        return lambda i: (_mod_index(jnp.minimum(i // tiles_per_seq, n_groups), k), 0, 0)

    def rope_idx(i):
        return (jnp.where(i < n_lat_tiles, i % tiles_per_seq, tiles_per_seq), 0)

    if split_src:
        x_specs, x_args = _row_sources(xs[0], xs[1], tm, n_lat_tiles)
    else:
        x_specs, x_args = [pl.BlockSpec((tm, d), lambda i: (i, 0))], [xs[0]]

    ada_specs, ada_args, ada_out_specs, ada_out_shapes = [], [], [], []
    if ada_next is not None:
        cond, w_ada, b_ada3 = ada_next
        n = w_ada.shape[-1]
        tn = n // n_lat_tiles
        chunk = lambda i: jnp.minimum(i, n_lat_tiles - 1)
        ada_specs = [pl.BlockSpec((MOD_ROWS, d), lambda i: (0, 0)),
                     pl.BlockSpec((None, d, tn), lambda i: (l + 1, 0, chunk(i))),
                     pl.BlockSpec((None, 1, tn), lambda i: (l + 1, 0, chunk(i)))]
        ada_args = [cond, w_ada, b_ada3]
        ada_out_specs = [pl.BlockSpec((MOD_ROWS, tn), lambda i: (0, chunk(i)))]
        ada_out_shapes = [jax.ShapeDtypeStruct((MOD_ROWS, n), F32)]

    kern = functools.partial(_inproj_kernel, n_lat_tiles=n_lat_tiles, q_w=q_w, kv_w=kv_w,
                             split_src=split_src, next_ada=ada_next is not None)
    return pl.pallas_call(
        kern,
        grid=(n_rows // tm,),
        in_specs=x_specs + [
            pl.BlockSpec((None, 1, d), lambda i: (l, 0, 0)),
            pl.BlockSpec((None, 1, d), mod_idx(0)),
            pl.BlockSpec((None, 1, d), mod_idx(1)),
            pl.BlockSpec((tm, HEAD_DIM), rope_idx),
            pl.BlockSpec((tm, HEAD_DIM), rope_idx),
            pl.BlockSpec((d, in_w), lambda i: (0, 0), pipeline_mode=pl.Buffered(1)),
        ] + ada_specs,
        out_specs=[
            pl.BlockSpec((tm, q_w), lambda i: (i, 0)),
            pl.BlockSpec((tm, 2 * kv_w), lambda i: (i, 0)),
            pl.BlockSpec((tm, puz_w), lambda i: (i, 0)),
        ] + ada_out_specs,
        out_shape=[
            jax.ShapeDtypeStruct((n_rows, q_w), BF16),
            jax.ShapeDtypeStruct((n_rows, 2 * kv_w), BF16),
            jax.ShapeDtypeStruct((n_rows, puz_w), F32),
        ] + ada_out_shapes,
        scratch_shapes=[pltpu.VMEM((tm, d), BF16)],
        compiler_params=_cparams(("arbitrary",), VMEM_LIMIT if ada_next is None else FFN_VMEM_LIMIT),
        name="in_proj",
    )(*x_args, norm_g, modr, modr, cos, sin, w_in, *ada_args)


def _gelu(x):
    return 0.5 * x * (1.0 + lax.erf(x * (0.5 ** 0.5)))


def _mixer_kernel(sink_ref, q_ref, *refs, windowed, blocks_per_seq, n_kv_heads, pool_w, sgu_w):
    if windowed:
        kvp_ref, kvc_ref, kvn_ref = refs[:3]
        refs = refs[3:]
    kvx_ref, puz_ref, php_ref, phn_ref, pw_ref, ps_ref, sg_ref, sw_ref, sb_ref, o_ref, ext_ref = refs
    blk = WINDOW_BLOCK
    hd = HEAD_DIM
    n = pl.program_id(0) % blocks_per_seq
    nblk = blocks_per_seq
    has_prev = (n > 0).astype(jnp.int32)
    has_next = (n < nblk - 1).astype(jnp.int32)
    kv_w = n_kv_heads * hd
    q_w = kv_w * GROUP
    logit_scale = hd ** -0.5 * LOG2E

    def attend(keys, vals, tile_masks, kvh):
        nk = keys.shape[0]
        qs = jnp.concatenate(
            [q_ref[:, (kvh * GROUP + g) * hd:(kvh * GROUP + g + 1) * hd] for g in range(GROUP)],
            axis=0)
        s = lax.dot_general(qs, keys, (((1,), (1,)), ((), ())),
                            preferred_element_type=F32) * logit_scale
        vals1 = jnp.concatenate([vals, jnp.ones((nk, hd), BF16)], axis=1)
        es, sink_terms = [], []
        for g in range(GROUP):
            tiles = []
            for t, mask in enumerate(tile_masks):
                st = s[g * blk:(g + 1) * blk, t * blk:(t + 1) * blk]
                tiles.append(st if mask is None else jnp.where(mask, st, NEG))
            mt = tiles[0]
            for st in tiles[1:]:
                mt = jnp.maximum(mt, st)
            snk = sink_ref[kvh * GROUP + g] * LOG2E
            m = jnp.maximum(jnp.max(mt, axis=-1, keepdims=True), snk)
            es.append(jnp.concatenate([jnp.exp2(st - m).astype(BF16) for st in tiles], axis=1))
            sink_terms.append(jnp.exp2(snk - m))
        o = _dot(jnp.concatenate(es, axis=0), vals1)
        for g in range(GROUP):
            h = kvh * GROUP + g
            og = o[g * blk:(g + 1) * blk]
            o_ref[:, h * hd:(h + 1) * hd] = (og[:, :hd] / (og[:, hd:] + sink_terms[g])).astype(BF16)

    if windowed:
        qi = lax.broadcasted_iota(jnp.int32, (blk, blk), 0)
        kj = lax.broadcasted_iota(jnp.int32, (blk, blk), 1)
        mask_prev = kj >= qi + (1 - has_prev) * blk
        mask_next = kj <= qi - (1 - has_next) * blk
        masks = [mask_prev, None, mask_next] + [None] * (kvx_ref.shape[0] // blk)
        for kvh in range(n_kv_heads):
            ks = slice(kvh * hd, (kvh + 1) * hd)
            vs = slice(kv_w + kvh * hd, kv_w + (kvh + 1) * hd)
            keys = jnp.concatenate([kvp_ref[:, ks], kvc_ref[:, ks], kvn_ref[:, ks], kvx_ref[:, ks]], axis=0)
            vals = jnp.concatenate([kvp_ref[:, vs], kvc_ref[:, vs], kvn_ref[:, vs], kvx_ref[:, vs]], axis=0)
            attend(keys, vals, masks, kvh)
    else:
        masks = [None] * (kvx_ref.shape[0] // blk)
        for kvh in range(n_kv_heads):
            attend(kvx_ref[:, kvh * hd:(kvh + 1) * hd],
                   kvx_ref[:, kv_w + kvh * hd:kv_w + (kvh + 1) * hd], masks, kvh)

    halo = POOL_HALO
    keep_prev = jnp.full((halo, pool_w), has_prev, jnp.int32) > 0
    keep_next = jnp.full((halo, pool_w), has_next, jnp.int32) > 0
    ext_ref[0:halo, :] = jnp.where(keep_prev, php_ref[...], 0.0)
    ext_ref[halo + blk:, :] = jnp.where(keep_next, phn_ref[...], 0.0)
    ext_ref[halo:halo + blk, :] = puz_ref[:, :pool_w]
    pos = n * blk + lax.broadcasted_iota(jnp.int32, (blk, hd), 0)
    seq_len = nblk * blk
    for g, win in enumerate(POOL_WINDOWS):
        cs = slice(g * hd, (g + 1) * hd)
        half = win // 2
        acc = ext_ref[halo - half:halo - half + blk, cs]
        for d in range(-half + 1, half):
            acc = acc + ext_ref[halo + d:halo + d + blk, cs]
        cnt = (jnp.minimum(pos + half, seq_len) - jnp.maximum(pos - half, 0)).astype(F32)
        pooled = acc / cnt - puz_ref[:, cs]
        y = _dot(pooled.astype(BF16), pw_ref[g]) * ps_ref[:, cs]
        o_ref[:, q_w + g * hd:q_w + (g + 1) * hd] = y.astype(BF16)

    for g in range(sgu_w // hd):
        cs = slice(g * hd, (g + 1) * hd)
        u = _gelu(puz_ref[:, pool_w + g * hd:pool_w + (g + 1) * hd])
        z = _gelu(puz_ref[:, pool_w + sgu_w + g * hd:pool_w + sgu_w + (g + 1) * hd])
        mu = jnp.mean(z, axis=-1, keepdims=True)
        zc = z - mu
        var = jnp.mean(zc * zc, axis=-1, keepdims=True)
        zn = zc * lax.rsqrt(var + EPS) * sg_ref[:, cs]
        mixed = _dot(sw_ref[g], zn.astype(BF16)) + sb_ref[g]
        o_ref[:, q_w + pool_w + g * hd:q_w + pool_w + (g + 1) * hd] = (u * mixed).astype(BF16)


def _mixer(q, kv, puz, sink, pool_w_l, pool_scale_l, sgu_g_l, sgu_w_l, sgu_b_l, *,
           n_lat_rows, seq, ctx_len, windowed):
    blk = WINDOW_BLOCK
    q_w = q.shape[1]
    kv_w = kv.shape[1] // 2
    pool_w = pool_scale_l.shape[-1]
    sgu_w = sgu_g_l.shape[-1]
    d_mix = q_w + pool_w + sgu_w
    per_seq = (seq if windowed else ctx_len) // blk
    base = 0 if windowed else n_lat_rows // blk
    rows = n_lat_rows if windowed else q.shape[0] - n_lat_rows
    first_ctx_block = n_lat_rows // ctx_len
    hpb = blk // POOL_HALO

    def cur(i):
        return (base + i, 0)

    def prev_idx(i):
        return (base + jnp.where(i % per_seq > 0, i - 1, i), 0)

    def next_idx(i):
        return (base + jnp.where(i % per_seq < per_seq - 1, i + 1, i), 0)

    def ctx_idx(i):
        return (first_ctx_block + i // per_seq, 0)

    def halo_prev_idx(i):
        return (jnp.where(i % per_seq > 0, (base + i) * hpb - 1, (base + i) * hpb), 0)

    def halo_next_idx(i):
        return (jnp.where(i % per_seq < per_seq - 1, (base + i + 1) * hpb, (base + i) * hpb), 0)

    kern = functools.partial(
        _mixer_kernel, windowed=windowed, blocks_per_seq=per_seq,
        n_kv_heads=kv_w // HEAD_DIM, pool_w=pool_w, sgu_w=sgu_w)
    n_pool = pool_w // HEAD_DIM
    n_sgu = sgu_w // HEAD_DIM
    window_specs, window_args = [], []
    if windowed:
        window_specs = [pl.BlockSpec((blk, 2 * kv_w), prev_idx),
                        pl.BlockSpec((blk, 2 * kv_w), cur),
                        pl.BlockSpec((blk, 2 * kv_w), next_idx)]
        window_args = [kv, kv, kv]
    return pl.pallas_call(
        kern,
        grid=(rows // blk,),
        in_specs=[
            pl.BlockSpec(memory_space=pltpu.SMEM),
            pl.BlockSpec((blk, q_w), cur),
        ] + window_specs + [
            pl.BlockSpec((ctx_len, 2 * kv_w), ctx_idx),
            pl.BlockSpec((blk, pool_w + 2 * sgu_w), cur),
            pl.BlockSpec((POOL_HALO, pool_w), halo_prev_idx),
            pl.BlockSpec((POOL_HALO, pool_w), halo_next_idx),
            pl.BlockSpec((n_pool, HEAD_DIM, HEAD_DIM), lambda i: (0, 0, 0)),
            pl.BlockSpec((1, pool_w), lambda i: (0, 0)),
            pl.BlockSpec((1, sgu_w), lambda i: (0, 0)),
            pl.BlockSpec((n_sgu, blk, blk), lambda i: (0, 0, 0)),
            pl.BlockSpec((n_sgu, blk, HEAD_DIM), lambda i: (0, 0, 0)),
        ],
        out_specs=pl.BlockSpec((blk, d_mix), lambda i: (i, 0)),
        out_shape=jax.ShapeDtypeStruct((rows, d_mix), BF16),
        scratch_shapes=[pltpu.VMEM((blk + 2 * POOL_HALO, pool_w), F32)],
        compiler_params=_cparams(("parallel",)),
        name="mixer_lat" if windowed else "mixer_ctx",
    )(sink, q, *window_args, kv, puz, puz, puz, pool_w_l, pool_scale_l, sgu_g_l, sgu_w_l, sgu_b_l)


def _outproj_kernel(*refs, n_lat_tiles, split_mix, split_x):
    refs = list(refs)
    ma_ref = refs.pop(0)
    mb_ref = refs.pop(0) if split_mix else None
    xa_ref = refs.pop(0)
    xb_ref = refs.pop(0) if split_x else None
    gate_ref, w_ref, o_ref = refs
    is_lat = (pl.program_id(0) < n_lat_tiles).astype(jnp.int32)
    x = _pick_rows(xa_ref, xb_ref, slice(None), is_lat)
    mix = _pick_rows(ma_ref, mb_ref, slice(None), is_lat)
    o_ref[...] = x + gate_ref[...] * _dot(mix, w_ref[...])


def _outproj(mixes, xs, modr, w_out, l, *, seq, n_lat_rows):
    rows = sum(m.shape[0] for m in mixes)
    d_mix = mixes[0].shape[1]
    d = xs[0].shape[1]
    tm = ROW_TILE
    tiles_per_seq = seq // tm
    n_lat_tiles = n_lat_rows // tm
    n_groups = n_lat_rows // seq

    def sources(arrs):
        if len(arrs) == 2:
            return _row_sources(arrs[0], arrs[1], tm, n_lat_tiles)
        return [pl.BlockSpec((tm, arrs[0].shape[1]), lambda i: (i, 0))], [arrs[0]]

    mix_specs, mix_args = sources(mixes)
    x_specs, x_args = sources(xs)
    kern = functools.partial(_outproj_kernel, n_lat_tiles=n_lat_tiles,
                             split_mix=len(mixes) == 2, split_x=len(xs) == 2)
    return pl.pallas_call(
        kern,
        grid=(rows // tm,),
        in_specs=mix_specs + x_specs + [
            pl.BlockSpec((None, 1, d),
                         lambda i: (_mod_index(jnp.minimum(i // tiles_per_seq, n_groups), 2), 0, 0)),
            pl.BlockSpec((d_mix, d), lambda i: (0, 0), pipeline_mode=pl.Buffered(1)),
        ],
        out_specs=pl.BlockSpec((tm, d), lambda i: (i, 0)),
        out_shape=jax.ShapeDtypeStruct((rows, d), F32),
        compiler_params=_cparams(("parallel",)),
        name="out_proj",
    )(*mix_args, *x_args, modr, w_out)


def _ffn_kernel(*refs, final_norm, n_cast):
    (x_ref, g_ref, sh_ref, sc_ref, gate_ref, wg_ref, wu_ref, wd_ref, fg_ref), refs = refs[:9], refs[9:]
    cast_src, refs = refs[:n_cast], refs[n_cast:]
    o_ref, refs = refs[0], refs[1:]
    cast_dst, (xn_ref,) = refs[:n_cast], refs[n_cast:]
    f = pl.program_id(1)
    nf = pl.num_programs(1)
    tm, d = x_ref.shape
    n_chunks = tm // NORM_CHUNK

    @pl.when(f == 0)
    def _():
        gain = g_ref[...] * (1.0 + sc_ref[...])
        shift = sh_ref[...]

        def body(r, carry):
            rows = pl.ds(pl.multiple_of(r * NORM_CHUNK, NORM_CHUNK), NORM_CHUNK)
            h = _modulated_norm(x_ref[rows, :], gain, shift)
            xn_ref[rows, :] = h.astype(BF16)
            o_ref[rows, :] = jnp.zeros((NORM_CHUNK, d), F32)
            return carry
        lax.fori_loop(0, n_chunks, body, 0, unroll=NORM_UNROLL)

    xn = xn_ref[...]
    gg = _dot(xn, wg_ref[...])
    uu = _dot(xn, wu_ref[...])
    act = (gg * jax.nn.sigmoid(gg) * uu).astype(BF16)
    casts = list(zip(cast_src, cast_dst))
    n_down = d // FFN_COL_TILE
    for cidx in range(n_down):
        for src, dst in casts[cidx::n_down]:
            dst[...] = src[...].astype(BF16)
        cs = slice(cidx * FFN_COL_TILE, (cidx + 1) * FFN_COL_TILE)
        o_ref[:, cs] += _dot(act, wd_ref[:, cs])

    @pl.when(f == nf - 1)
    def _():
        def body(r, carry):
            rows = pl.ds(pl.multiple_of(r * NORM_CHUNK, NORM_CHUNK), NORM_CHUNK)
            y = x_ref[rows, :] + gate_ref[...] * o_ref[rows, :]
            if final_norm:
                ms = jnp.mean(y * y, axis=-1, keepdims=True)
                y = y * lax.rsqrt(ms + EPS) * fg_ref[...]
            o_ref[rows, :] = y
            return carry
        lax.fori_loop(0, n_chunks, body, 0, unroll=NORM_UNROLL)


def _cast_chunks(weights, l_next, n_i, n_f):
    def split(n, unit, limit):
        return max(k for k in range(1, limit + 1) if n % (k * unit) == 0)

    in_specs, out_specs, out_shapes = [], [], []
    for w in weights:
        _, r, c = w.shape
        rows_i = (split(r, BF16_SUBLANES, n_i), split(c, LANES, n_f))
        rows_f = (split(r, BF16_SUBLANES, n_f), split(c, LANES, n_i))
        if rows_i[0] * rows_i[1] >= rows_f[0] * rows_f[1]:
            kr, kc = rows_i
            idx = lambda i, f, kr=kr, kc=kc: (
                jnp.minimum(i, kr - 1), jnp.where(i < kr, jnp.minimum(f, kc - 1), kc - 1))
        else:
            kr, kc = rows_f
            idx = lambda i, f, kr=kr, kc=kc: (
                jnp.where(i < kc, jnp.minimum(f, kr - 1), kr - 1), jnp.minimum(i, kc - 1))
        blk = (r // kr, c // kc)
        in_specs.append(pl.BlockSpec((None,) + blk, lambda i, f, idx=idx: (l_next,) + idx(i, f)))
        out_specs.append(pl.BlockSpec(blk, idx))
        out_shapes.append(jax.ShapeDtypeStruct((r, c), BF16))
    return in_specs, out_specs, out_shapes


def _ffn(xt, norm_g, modr, w_gu, w_dn, final_g, l, *, rows, seq, n_lat_rows, final_norm, cast_next=()):
    d = xt.shape[1]
    d_ff = w_dn.shape[0]
    tm = FFN_ROW_TILE
    tf = FFN_COL_TILE
    nf = d_ff // tf
    tiles_per_seq = seq // tm
    n_groups = n_lat_rows // seq

    def mod_idx(k):
        return lambda i, f: (_mod_index(jnp.minimum(i // tiles_per_seq, n_groups), k), 0, 0)

    cast_in, cast_out, cast_shapes = _cast_chunks(cast_next, l + 1, rows // tm, nf)
    kern = functools.partial(_ffn_kernel, final_norm=final_norm, n_cast=len(cast_next))
    outs = pl.pallas_call(
        kern,
        grid=(rows // tm, nf),
        in_specs=[
            pl.BlockSpec((tm, d), lambda i, f: (i, 0)),
            pl.BlockSpec((None, 1, d), lambda i, f: (l, 0, 0)),
            pl.BlockSpec((None, 1, d), mod_idx(3)),
            pl.BlockSpec((None, 1, d), mod_idx(4)),
            pl.BlockSpec((None, 1, d), mod_idx(5)),
            pl.BlockSpec((d, tf), lambda i, f: (0, f)),
            pl.BlockSpec((d, tf), lambda i, f: (0, nf + f)),
            pl.BlockSpec((tf, d), lambda i, f: (f, 0)),
            pl.BlockSpec((1, d), lambda i, f: (0, 0)),
        ] + cast_in,
        out_specs=[pl.BlockSpec((tm, d), lambda i, f: (i, 0))] + cast_out,
        out_shape=[jax.ShapeDtypeStruct((rows, d), F32)] + cast_shapes,
        scratch_shapes=[pltpu.VMEM((tm, d), BF16)],
        compiler_params=_cparams(("arbitrary", "arbitrary"), FFN_VMEM_LIMIT),
        name="ffn",
    )(xt, norm_g, modr, modr, modr, w_gu, w_gu, w_dn, final_g, *cast_next)
    return outs[0], tuple(outs[1:])


def _rope_tables(seq, pad_rows):
    rows = seq // GRID_W
    inv_freq = ROPE_THETA ** (-jnp.arange(0, ROPE_AXIS, 2, dtype=F32) / ROPE_AXIS)
    row = jnp.repeat(jnp.arange(rows, dtype=F32), GRID_W)
    col = jnp.tile(jnp.arange(GRID_W, dtype=F32), rows)
    ar = row[:, None] * inv_freq
    ac = col[:, None] * inv_freq
    cos = jnp.concatenate([jnp.cos(ar), jnp.cos(ar), jnp.cos(ac), jnp.cos(ac)], axis=1)
    sin = jnp.concatenate([-jnp.sin(ar), jnp.sin(ar), -jnp.sin(ac), jnp.sin(ac)], axis=1)
    cos = jnp.concatenate([cos, jnp.ones((pad_rows, HEAD_DIM), F32)], axis=0)
    sin = jnp.concatenate([sin, jnp.zeros((pad_rows, HEAD_DIM), F32)], axis=0)
    return cos, sin


def kernel(x, c, ctx, c_ctx, w_ada, b_ada, norm_mix_g, norm_ffn_g, w_in, attn_sink, pool_w, pool_scale,
           sgu_norm_g, sgu_w, sgu_b, w_out, w_gate_up, w_down, final_norm_g):
    bsz, seq, d = x.shape
    ctx_len = ctx.shape[1]
    depth = w_in.shape[0]
    n_lat_rows = bsz * seq
    n_rows = n_lat_rows + bsz * ctx_len

    cond = jnp.concatenate([c, c_ctx[None, :], jnp.zeros((MOD_ROWS - bsz - 1, d), F32)], axis=0)
    b_ada3 = b_ada.reshape(depth, 1, MOD_CHUNKS * d)
    modr = _ada_table(cond, w_ada, b_ada3, 0).reshape(MOD_ROWS * MOD_CHUNKS, 1, d)

    cos, sin = _rope_tables(seq, ROW_TILE)
    big_weights = (w_in, w_out, w_gate_up, w_down)
    w_in_b, w_out_b, w_gu_b, w_dn_b = (w[0].astype(BF16) for w in big_weights)
    pool_w_b = pool_w.astype(BF16)
    sgu_w_b = sgu_w.astype(BF16)
    norm_mix = norm_mix_g.reshape(depth, 1, d)
    norm_ffn = norm_ffn_g.reshape(depth, 1, d)
    final_g = final_norm_g.reshape(1, d)

    xs = (x.reshape(n_lat_rows, d), ctx.reshape(bsz * ctx_len, d))
    for l in range(depth):
        last = l == depth - 1
        rows = n_lat_rows if last else n_rows
        q, kv, puz, *mod_next = _inproj(xs, norm_mix, modr, cos, sin, w_in_b, l,
                                        n_rows=n_rows, n_lat_rows=n_lat_rows, seq=seq,
                                        ada_next=None if last else (cond, w_ada, b_ada3))
        sgu_b_full = jnp.broadcast_to(sgu_b[l][:, :, None], sgu_b.shape[1:] + (HEAD_DIM,))
        mixes = tuple(
            _mixer(q, kv, puz, attn_sink[l], pool_w_b[l], pool_scale[l].reshape(1, -1),
                   sgu_norm_g[l].reshape(1, -1), sgu_w_b[l], sgu_b_full,
                   n_lat_rows=n_lat_rows, seq=seq, ctx_len=ctx_len, windowed=windowed)
            for windowed in ((True,) if last else (True, False)))
        xt = _outproj(mixes, xs, modr, w_out_b, l, seq=seq, n_lat_rows=n_lat_rows)
        xt, next_w = _ffn(xt, norm_ffn, modr, w_gu_b, w_dn_b, final_g, l, rows=rows, seq=seq,
                          n_lat_rows=n_lat_rows, final_norm=last,
                          cast_next=() if last else big_weights)
        if not last:
            w_in_b, w_out_b, w_gu_b, w_dn_b = next_w
            modr = mod_next[0].reshape(MOD_ROWS * MOD_CHUNKS, 1, d)
        xs = (xt,)
    return xt.reshape(bsz, seq, d)
```

```python
import functools

import jax
import jax.numpy as jnp
from jax import lax
from jax.experimental import pallas as pl
from jax.experimental.pallas import tpu as pltpu

F32 = jnp.float32
BF16 = jnp.bfloat16

GRID_W = 64
HEAD_DIM = 128
GROUP = 4
WINDOW_BLOCK = 128
ROPE_THETA = 10000.0
ROPE_AXIS = HEAD_DIM // 2
POOL_WINDOWS = (2, 4, 8, 16)
POOL_HALO = 8
EPS = 1e-6
NEG = -1e30
LOG2E = 1.4426950408889634

MOD_ROWS = 8
MOD_CHUNKS = 6

LANES = 128
BF16_SUBLANES = 16
VMEM_LIMIT = 56 * 1024 * 1024
FFN_VMEM_LIMIT = 62 * 1024 * 1024
ROW_TILE = 512
INPROJ_SUBTILES = 2
FFN_FIRST_SUBTILES = 2
FFN_ROW_TILE = 1024
FFN_COL_TILE = 512
NORM_CHUNK = 32
NORM_UNROLL = 4


def _cparams(sem, vmem_limit=VMEM_LIMIT):
    return pltpu.CompilerParams(dimension_semantics=sem, vmem_limit_bytes=vmem_limit)


def _dot(a, b):
    return jnp.dot(a, b, preferred_element_type=F32)


def _mod_index(group, chunk):
    return group * MOD_CHUNKS + chunk


def _ada_chunk(c_ref, w_ref, b_ref):
    cv = c_ref[...]
    a = (cv * jax.nn.sigmoid(cv)).astype(BF16)
    return _dot(a, w_ref[...].astype(BF16)) + b_ref[...]


def _ada_kernel(c_ref, w_ref, b_ref, o_ref):
    o_ref[...] = _ada_chunk(c_ref, w_ref, b_ref)


def _ada_table(cond, w_ada, b_ada3, l):
    _, d, n = w_ada.shape
    tn = 1024
    return pl.pallas_call(
        _ada_kernel,
        grid=(n // tn,),
        in_specs=[
            pl.BlockSpec((MOD_ROWS, d), lambda j: (0, 0)),
            pl.BlockSpec((None, d, tn), lambda j: (l, 0, j)),
            pl.BlockSpec((None, 1, tn), lambda j: (l, 0, j)),
        ],
        out_specs=pl.BlockSpec((MOD_ROWS, tn), lambda j: (0, j)),
        out_shape=jax.ShapeDtypeStruct((MOD_ROWS, n), F32),
        compiler_params=_cparams(("parallel",)),
        name="ada_table",
    )(cond, w_ada, b_ada3)


def _modulated_norm(x, gain, shift):
    ms = jnp.mean(x * x, axis=-1, keepdims=True)
    return x * lax.rsqrt(ms + EPS) * gain + shift


def _pick_rows(lat_ref, ctx_ref, rows, is_lat_tile):
    if ctx_ref is None:
        return lat_ref[rows, :]
    a = lat_ref[rows, :]
    sel = jnp.full(a.shape, is_lat_tile, jnp.int32) > 0
    return jnp.where(sel, a, ctx_ref[rows, :])


def _inproj_kernel(*refs, n_lat_tiles, q_w, kv_w, split_src, next_ada):
    refs = list(refs)
    xa_ref = refs.pop(0)
    xb_ref = refs.pop(0) if split_src else None
    g_ref, sh_ref, sc_ref, cos_ref, sin_ref, w_ref = refs[:6]
    refs = refs[6:]
    if next_ada:
        cond_ref, wada_ref, bada_ref = refs[:3]
        q_ref, kv_ref, puz_ref, modn_ref, hb_ref = refs[3:]
    else:
        q_ref, kv_ref, puz_ref, hb_ref = refs
    i = pl.program_id(0)
    tm = hb_ref.shape[0]
    is_lat = (i < n_lat_tiles).astype(jnp.int32)

    gain = g_ref[...] * (1.0 + sc_ref[...])
    shift = sh_ref[...]

    sub = tm // INPROJ_SUBTILES
    lane = lax.broadcasted_iota(jnp.int32, (sub, HEAD_DIM), 1)
    first_half = (lane % ROPE_AXIS) < (ROPE_AXIS // 2)

    for s in range(INPROJ_SUBTILES):
        for c in range(sub // NORM_CHUNK):
            rows = slice(s * sub + c * NORM_CHUNK, s * sub + (c + 1) * NORM_CHUNK)
            xs = _pick_rows(xa_ref, xb_ref, rows, is_lat)
            hb_ref[rows, :] = _modulated_norm(xs, gain, shift).astype(BF16)
        rs = slice(s * sub, (s + 1) * sub)
        if next_ada and s == 0:
            modn_ref[...] = _ada_chunk(cond_ref, wada_ref, bada_ref)
        hb = hb_ref[rs, :]
        pq = _dot(hb, w_ref[:, :q_w])
        pkv = _dot(hb, w_ref[:, q_w:q_w + 2 * kv_w])
        puz_ref[rs, :] = _dot(hb, w_ref[:, q_w + 2 * kv_w:])

        cos = cos_ref[rs, :]
        sin = sin_ref[rs, :]

        def rope(t):
            rot = jnp.where(first_half,
                            pltpu.roll(t, HEAD_DIM - ROPE_AXIS // 2, 1),
                            pltpu.roll(t, ROPE_AXIS // 2, 1))
            return t * cos + rot * sin

        for hd in range(q_w // HEAD_DIM):
            sl = slice(hd * HEAD_DIM, (hd + 1) * HEAD_DIM)
            q_ref[rs, sl] = rope(pq[:, sl]).astype(BF16)
        for hd in range(kv_w // HEAD_DIM):
            sl = slice(hd * HEAD_DIM, (hd + 1) * HEAD_DIM)
            kv_ref[rs, sl] = rope(pkv[:, sl]).astype(BF16)
        kv_ref[rs, kv_w:] = pkv[:, kv_w:].astype(BF16)


def _row_sources(x_lat, x_ctx, tm, n_lat_tiles):
    d = x_lat.shape[1]
    lat_map = lambda i: (jnp.minimum(i, n_lat_tiles - 1), 0)
    ctx_map = lambda i: (jnp.maximum(i - n_lat_tiles, 0), 0)
    return [pl.BlockSpec((tm, d), lat_map), pl.BlockSpec((tm, d), ctx_map)], [x_lat, x_ctx]


def _inproj(xs, norm_g, modr, cos, sin, w_in, l, *, n_rows, n_lat_rows, seq, ada_next=None):
    d = xs[0].shape[1]
    in_w = w_in.shape[-1]
    q_w = d // 2
    kv_w = q_w // GROUP
    puz_w = in_w - q_w - 2 * kv_w
    tm = ROW_TILE
    tiles_per_seq = seq // tm
    n_lat_tiles = n_lat_rows // tm
    n_groups = n_lat_tiles // tiles_per_seq
    split_src = len(xs) == 2

    def mod_idx(k):
        return lambda i: (_mod_index(jnp.minimum(i // tiles_per_seq, n_groups), k), 0, 0)

    def rope_idx(i):
        return (jnp.where(i < n_lat_tiles, i % tiles_per_seq, tiles_per_seq), 0)

    if split_src:
        x_specs, x_args = _row_sources(xs[0], xs[1], tm, n_lat_tiles)
    else:
        x_specs, x_args = [pl.BlockSpec((tm, d), lambda i: (i, 0))], [xs[0]]

    ada_specs, ada_args, ada_out_specs, ada_out_shapes = [], [], [], []
    if ada_next is not None:
        cond, w_ada, b_ada3 = ada_next
        n = w_ada.shape[-1]
        tn = n // n_lat_tiles
        chunk = lambda i: jnp.minimum(i, n_lat_tiles - 1)
        ada_specs = [pl.BlockSpec((MOD_ROWS, d), lambda i: (0, 0)),
                     pl.BlockSpec((None, d, tn), lambda i: (l + 1, 0, chunk(i))),
                     pl.BlockSpec((None, 1, tn), lambda i: (l + 1, 0, chunk(i)))]
        ada_args = [cond, w_ada, b_ada3]
        ada_out_specs = [pl.BlockSpec((MOD_ROWS, tn), lambda i: (0, chunk(i)))]
        ada_out_shapes = [jax.ShapeDtypeStruct((MOD_ROWS, n), F32)]

    kern = functools.partial(_inproj_kernel, n_lat_tiles=n_lat_tiles, q_w=q_w, kv_w=kv_w,
                             split_src=split_src, next_ada=ada_next is not None)
    return pl.pallas_call(
        kern,
        grid=(n_rows // tm,),
        in_specs=x_specs + [
            pl.BlockSpec((None, 1, d), lambda i: (l, 0, 0)),
            pl.BlockSpec((None, 1, d), mod_idx(0)),
            pl.BlockSpec((None, 1, d), mod_idx(1)),
            pl.BlockSpec((tm, HEAD_DIM), rope_idx),
            pl.BlockSpec((tm, HEAD_DIM), rope_idx),
            pl.BlockSpec((d, in_w), lambda i: (0, 0), pipeline_mode=pl.Buffered(1)),
        ] + ada_specs,
        out_specs=[
            pl.BlockSpec((tm, q_w), lambda i: (i, 0)),
            pl.BlockSpec((tm, 2 * kv_w), lambda i: (i, 0)),
            pl.BlockSpec((tm, puz_w), lambda i: (i, 0)),
        ] + ada_out_specs,
        out_shape=[
            jax.ShapeDtypeStruct((n_rows, q_w), BF16),
            jax.ShapeDtypeStruct((n_rows, 2 * kv_w), BF16),
            jax.ShapeDtypeStruct((n_rows, puz_w), F32),
        ] + ada_out_shapes,
        scratch_shapes=[pltpu.VMEM((tm, d), BF16)],
        compiler_params=_cparams(("arbitrary",), VMEM_LIMIT if ada_next is None else FFN_VMEM_LIMIT),
        name="in_proj",
    )(*x_args, norm_g, modr, modr, cos, sin, w_in, *ada_args)


def _gelu(x):
    return 0.5 * x * (1.0 + lax.erf(x * (0.5 ** 0.5)))


def _mixer_kernel(sink_ref, q_ref, *refs, windowed, blocks_per_seq, n_kv_heads, pool_w, sgu_w):
    if windowed:
        kvp_ref, kvc_ref, kvn_ref = refs[:3]
        refs = refs[3:]
    kvx_ref, puz_ref, php_ref, phn_ref, pw_ref, ps_ref, sg_ref, sw_ref, sb_ref, o_ref, ext_ref = refs
    blk = WINDOW_BLOCK
    hd = HEAD_DIM
    n = pl.program_id(0) % blocks_per_seq
    nblk = blocks_per_seq
    has_prev = (n > 0).astype(jnp.int32)
    has_next = (n < nblk - 1).astype(jnp.int32)
    kv_w = n_kv_heads * hd
    q_w = kv_w * GROUP
    logit_scale = hd ** -0.5 * LOG2E

    def attend(keys, vals, tile_masks, kvh):
        nk = keys.shape[0]
        qs = jnp.concatenate(
            [q_ref[:, (kvh * GROUP + g) * hd:(kvh * GROUP + g + 1) * hd] for g in range(GROUP)],
            axis=0)
        s = lax.dot_general(qs, keys, (((1,), (1,)), ((), ())),
                            preferred_element_type=F32) * logit_scale
        vals1 = jnp.concatenate([vals, jnp.ones((nk, hd), BF16)], axis=1)
        es, sink_terms = [], []
        for g in range(GROUP):
            tiles = []
            for t, mask in enumerate(tile_masks):
                st = s[g * blk:(g + 1) * blk, t * blk:(t + 1) * blk]
                tiles.append(st if mask is None else jnp.where(mask, st, NEG))
            mt = tiles[0]
            for st in tiles[1:]:
                mt = jnp.maximum(mt, st)
            snk = sink_ref[kvh * GROUP + g] * LOG2E
            m = jnp.maximum(jnp.max(mt, axis=-1, keepdims=True), snk)
            es.append(jnp.concatenate([jnp.exp2(st - m).astype(BF16) for st in tiles], axis=1))
            sink_terms.append(jnp.exp2(snk - m))
        o = _dot(jnp.concatenate(es, axis=0), vals1)
        for g in range(GROUP):
            h = kvh * GROUP + g
            og = o[g * blk:(g + 1) * blk]
            o_ref[:, h * hd:(h + 1) * hd] = (og[:, :hd] / (og[:, hd:] + sink_terms[g])).astype(BF16)

    if windowed:
        qi = lax.broadcasted_iota(jnp.int32, (blk, blk), 0)
        kj = lax.broadcasted_iota(jnp.int32, (blk, blk), 1)
        mask_prev = kj >= qi + (1 - has_prev) * blk
        mask_next = kj <= qi - (1 - has_next) * blk
        masks = [mask_prev, None, mask_next] + [None] * (kvx_ref.shape[0] // blk)
        for kvh in range(n_kv_heads):
            ks = slice(kvh * hd, (kvh + 1) * hd)
            vs = slice(kv_w + kvh * hd, kv_w + (kvh + 1) * hd)
            keys = jnp.concatenate([kvp_ref[:, ks], kvc_ref[:, ks], kvn_ref[:, ks], kvx_ref[:, ks]], axis=0)
            vals = jnp.concatenate([kvp_ref[:, vs], kvc_ref[:, vs], kvn_ref[:, vs], kvx_ref[:, vs]], axis=0)
            attend(keys, vals, masks, kvh)
    else:
        masks = [None] * (kvx_ref.shape[0] // blk)
        for kvh in range(n_kv_heads):
            attend(kvx_ref[:, kvh * hd:(kvh + 1) * hd],
                   kvx_ref[:, kv_w + kvh * hd:kv_w + (kvh + 1) * hd], masks, kvh)

    halo = POOL_HALO
    keep_prev = jnp.full((halo, pool_w), has_prev, jnp.int32) > 0
    keep_next = jnp.full((halo, pool_w), has_next, jnp.int32) > 0
    ext_ref[0:halo, :] = jnp.where(keep_prev, php_ref[...], 0.0)
    ext_ref[halo + blk:, :] = jnp.where(keep_next, phn_ref[...], 0.0)
    ext_ref[halo:halo + blk, :] = puz_ref[:, :pool_w]
    pos = n * blk + lax.broadcasted_iota(jnp.int32, (blk, hd), 0)
    seq_len = nblk * blk
    for g, win in enumerate(POOL_WINDOWS):
        cs = slice(g * hd, (g + 1) * hd)
        half = win // 2
        acc = ext_ref[halo - half:halo - half + blk, cs]
        for d in range(-half + 1, half):
            acc = acc + ext_ref[halo + d:halo + d + blk, cs]
        cnt = (jnp.minimum(pos + half, seq_len) - jnp.maximum(pos - half, 0)).astype(F32)
        pooled = acc / cnt - puz_ref[:, cs]
        y = _dot(pooled.astype(BF16), pw_ref[g]) * ps_ref[:, cs]
        o_ref[:, q_w + g * hd:q_w + (g + 1) * hd] = y.astype(BF16)

    for g in range(sgu_w // hd):
        cs = slice(g * hd, (g + 1) * hd)
        u = _gelu(puz_ref[:, pool_w + g * hd:pool_w + (g + 1) * hd])
        z = _gelu(puz_ref[:, pool_w + sgu_w + g * hd:pool_w + sgu_w + (g + 1) * hd])
        mu = jnp.mean(z, axis=-1, keepdims=True)
        zc = z - mu
        var = jnp.mean(zc * zc, axis=-1, keepdims=True)
        zn = zc * lax.rsqrt(var + EPS) * sg_ref[:, cs]
        mixed = _dot(sw_ref[g], zn.astype(BF16)) + sb_ref[g]
        o_ref[:, q_w + pool_w + g * hd:q_w + pool_w + (g + 1) * hd] = (u * mixed).astype(BF16)


def _mixer(q, kv, puz, sink, pool_w_l, pool_scale_l, sgu_g_l, sgu_w_l, sgu_b_l, *,
           n_lat_rows, seq, ctx_len, windowed):
    blk = WINDOW_BLOCK
    q_w = q.shape[1]
    kv_w = kv.shape[1] // 2
    pool_w = pool_scale_l.shape[-1]
    sgu_w = sgu_g_l.shape[-1]
    d_mix = q_w + pool_w + sgu_w
    per_seq = (seq if windowed else ctx_len) // blk
    base = 0 if windowed else n_lat_rows // blk
    rows = n_lat_rows if windowed else q.shape[0] - n_lat_rows
    first_ctx_block = n_lat_rows // ctx_len
    hpb = blk // POOL_HALO

    def cur(i):
        return (base + i, 0)

    def prev_idx(i):
        return (base + jnp.where(i % per_seq > 0, i - 1, i), 0)

    def next_idx(i):
        return (base + jnp.where(i % per_seq < per_seq - 1, i + 1, i), 0)

    def ctx_idx(i):
        return (first_ctx_block + i // per_seq, 0)

    def halo_prev_idx(i):
        return (jnp.where(i % per_seq > 0, (base + i) * hpb - 1, (base + i) * hpb), 0)

    def halo_next_idx(i):
        return (jnp.where(i % per_seq < per_seq - 1, (base + i + 1) * hpb, (base + i) * hpb), 0)

    kern = functools.partial(
        _mixer_kernel, windowed=windowed, blocks_per_seq=per_seq,
        n_kv_heads=kv_w // HEAD_DIM, pool_w=pool_w, sgu_w=sgu_w)
    n_pool = pool_w // HEAD_DIM
    n_sgu = sgu_w // HEAD_DIM
    window_specs, window_args = [], []
    if windowed:
        window_specs = [pl.BlockSpec((blk, 2 * kv_w), prev_idx),
                        pl.BlockSpec((blk, 2 * kv_w), cur),
                        pl.BlockSpec((blk, 2 * kv_w), next_idx)]
        window_args = [kv, kv, kv]
    return pl.pallas_call(
        kern,
        grid=(rows // blk,),
        in_specs=[
            pl.BlockSpec(memory_space=pltpu.SMEM),
            pl.BlockSpec((blk, q_w), cur),
        ] + window_specs + [
            pl.BlockSpec((ctx_len, 2 * kv_w), ctx_idx),
            pl.BlockSpec((blk, pool_w + 2 * sgu_w), cur),
            pl.BlockSpec((POOL_HALO, pool_w), halo_prev_idx),
            pl.BlockSpec((POOL_HALO, pool_w), halo_next_idx),
            pl.BlockSpec((n_pool, HEAD_DIM, HEAD_DIM), lambda i: (0, 0, 0)),
            pl.BlockSpec((1, pool_w), lambda i: (0, 0)),
            pl.BlockSpec((1, sgu_w), lambda i: (0, 0)),
            pl.BlockSpec((n_sgu, blk, blk), lambda i: (0, 0, 0)),
            pl.BlockSpec((n_sgu, blk, HEAD_DIM), lambda i: (0, 0, 0)),
        ],
        out_specs=pl.BlockSpec((blk, d_mix), lambda i: (i, 0)),
        out_shape=jax.ShapeDtypeStruct((rows, d_mix), BF16),
        scratch_shapes=[pltpu.VMEM((blk + 2 * POOL_HALO, pool_w), F32)],
        compiler_params=_cparams(("parallel",)),
        name="mixer_lat" if windowed else "mixer_ctx",
    )(sink, q, *window_args, kv, puz, puz, puz, pool_w_l, pool_scale_l, sgu_g_l, sgu_w_l, sgu_b_l)


def _outproj_kernel(*refs, n_lat_tiles, split_mix, split_x):
    refs = list(refs)
    ma_ref = refs.pop(0)
    mb_ref = refs.pop(0) if split_mix else None
    xa_ref = refs.pop(0)
    xb_ref = refs.pop(0) if split_x else None
    gate_ref, w_ref, o_ref = refs
    is_lat = (pl.program_id(0) < n_lat_tiles).astype(jnp.int32)
    x = _pick_rows(xa_ref, xb_ref, slice(None), is_lat)
    mix = _pick_rows(ma_ref, mb_ref, slice(None), is_lat)
    o_ref[...] = x + gate_ref[...] * _dot(mix, w_ref[...])


def _outproj(mixes, xs, modr, w_out, l, *, seq, n_lat_rows):
    rows = sum(m.shape[0] for m in mixes)
    d_mix = mixes[0].shape[1]
    d = xs[0].shape[1]
    tm = ROW_TILE
    tiles_per_seq = seq // tm
    n_lat_tiles = n_lat_rows // tm
    n_groups = n_lat_rows // seq

    def sources(arrs):
        if len(arrs) == 2:
            return _row_sources(arrs[0], arrs[1], tm, n_lat_tiles)
        return [pl.BlockSpec((tm, arrs[0].shape[1]), lambda i: (i, 0))], [arrs[0]]

    mix_specs, mix_args = sources(mixes)
    x_specs, x_args = sources(xs)
    kern = functools.partial(_outproj_kernel, n_lat_tiles=n_lat_tiles,
                             split_mix=len(mixes) == 2, split_x=len(xs) == 2)
    return pl.pallas_call(
        kern,
        grid=(rows // tm,),
        in_specs=mix_specs + x_specs + [
            pl.BlockSpec((None, 1, d),
                         lambda i: (_mod_index(jnp.minimum(i // tiles_per_seq, n_groups), 2), 0, 0)),
            pl.BlockSpec((d_mix, d), lambda i: (0, 0), pipeline_mode=pl.Buffered(1)),
        ],
        out_specs=pl.BlockSpec((tm, d), lambda i: (i, 0)),
        out_shape=jax.ShapeDtypeStruct((rows, d), F32),
        compiler_params=_cparams(("parallel",)),
        name="out_proj",
    )(*mix_args, *x_args, modr, w_out)


def _ffn_kernel(*refs, final_norm, n_cast):
    (x_ref, g_ref, sh_ref, sc_ref, gate_ref, wg_ref, wu_ref, wd_ref, fg_ref), refs = refs[:9], refs[9:]
    cast_src, refs = refs[:n_cast], refs[n_cast:]
    o_ref, refs = refs[0], refs[1:]
    cast_dst, (xn_ref,) = refs[:n_cast], refs[n_cast:]
    f = pl.program_id(1)
    nf = pl.num_programs(1)
    tm, d = x_ref.shape
    n_chunks = tm // NORM_CHUNK

    casts = list(zip(cast_src, cast_dst))
    n_down = d // FFN_COL_TILE

    def swiglu_rows(rs, first, with_casts):
        xn = xn_ref[rs, :]
        gg = _dot(xn, wg_ref[...])
        uu = _dot(xn, wu_ref[...])
        act = (gg * jax.nn.sigmoid(gg) * uu).astype(BF16)
        for cidx in range(n_down):
            if with_casts:
                for src, dst in casts[cidx::n_down]:
                    dst[...] = src[...].astype(BF16)
            cs = slice(cidx * FFN_COL_TILE, (cidx + 1) * FFN_COL_TILE)
            down = _dot(act, wd_ref[:, cs])
            if first:
                o_ref[rs, cs] = down
            else:
                o_ref[rs, cs] += down

    @pl.when(f == 0)
    def _():
        gain = g_ref[...] * (1.0 + sc_ref[...])
        shift = sh_ref[...]
        sub = tm // FFN_FIRST_SUBTILES
        for s in range(FFN_FIRST_SUBTILES):
            for c in range(sub // NORM_CHUNK):
                rows = slice(s * sub + c * NORM_CHUNK, s * sub + (c + 1) * NORM_CHUNK)
                xn_ref[rows, :] = _modulated_norm(x_ref[rows, :], gain, shift).astype(BF16)
            swiglu_rows(slice(s * sub, (s + 1) * sub), first=True, with_casts=s == 0)

    @pl.when(f > 0)
    def _():
        swiglu_rows(slice(None), first=False, with_casts=True)

    @pl.when(f == nf - 1)
    def _():
        def body(r, carry):
            rows = pl.ds(pl.multiple_of(r * NORM_CHUNK, NORM_CHUNK), NORM_CHUNK)
            y = x_ref[rows, :] + gate_ref[...] * o_ref[rows, :]
            if final_norm:
                ms = jnp.mean(y * y, axis=-1, keepdims=True)
                y = y * lax.rsqrt(ms + EPS) * fg_ref[...]
            o_ref[rows, :] = y
            return carry
        lax.fori_loop(0, n_chunks, body, 0, unroll=NORM_UNROLL)


def _cast_chunks(weights, l_next, n_i, n_f):
    def split(n, unit, limit):
        return max(k for k in range(1, limit + 1) if n % (k * unit) == 0)

    in_specs, out_specs, out_shapes = [], [], []
    for w in weights:
        _, r, c = w.shape
        rows_i = (split(r, BF16_SUBLANES, n_i), split(c, LANES, n_f))
        rows_f = (split(r, BF16_SUBLANES, n_f), split(c, LANES, n_i))
        if rows_i[0] * rows_i[1] >= rows_f[0] * rows_f[1]:
            kr, kc = rows_i
            idx = lambda i, f, kr=kr, kc=kc: (
                jnp.minimum(i, kr - 1), jnp.where(i < kr, jnp.minimum(f, kc - 1), kc - 1))
        else:
            kr, kc = rows_f
            idx = lambda i, f, kr=kr, kc=kc: (
                jnp.where(i < kc, jnp.minimum(f, kr - 1), kr - 1), jnp.minimum(i, kc - 1))
        blk = (r // kr, c // kc)
        in_specs.append(pl.BlockSpec((None,) + blk, lambda i, f, idx=idx: (l_next,) + idx(i, f)))
        out_specs.append(pl.BlockSpec(blk, idx))
        out_shapes.append(jax.ShapeDtypeStruct((r, c), BF16))
    return in_specs, out_specs, out_shapes


def _ffn(xt, norm_g, modr, w_gu, w_dn, final_g, l, *, rows, seq, n_lat_rows, final_norm, cast_next=()):
    d = xt.shape[1]
    d_ff = w_dn.shape[0]
    tm = FFN_ROW_TILE
    tf = FFN_COL_TILE
    nf = d_ff // tf
    tiles_per_seq = seq // tm
    n_groups = n_lat_rows // seq

    def mod_idx(k):
        return lambda i, f: (_mod_index(jnp.minimum(i // tiles_per_seq, n_groups), k), 0, 0)

    cast_in, cast_out, cast_shapes = _cast_chunks(cast_next, l + 1, rows // tm, nf)
    kern = functools.partial(_ffn_kernel, final_norm=final_norm, n_cast=len(cast_next))
    outs = pl.pallas_call(
        kern,
        grid=(rows // tm, nf),
        in_specs=[
            pl.BlockSpec((tm, d), lambda i, f: (i, 0)),
            pl.BlockSpec((None, 1, d), lambda i, f: (l, 0, 0)),
            pl.BlockSpec((None, 1, d), mod_idx(3)),
            pl.BlockSpec((None, 1, d), mod_idx(4)),
            pl.BlockSpec((None, 1, d), mod_idx(5)),
            pl.BlockSpec((d, tf), lambda i, f: (0, f)),
            pl.BlockSpec((d, tf), lambda i, f: (0, nf + f)),
            pl.BlockSpec((tf, d), lambda i, f: (f, 0)),
            pl.BlockSpec((1, d), lambda i, f: (0, 0)),
        ] + cast_in,
        out_specs=[pl.BlockSpec((tm, d), lambda i, f: (i, 0))] + cast_out,
        out_shape=[jax.ShapeDtypeStruct((rows, d), F32)] + cast_shapes,
        scratch_shapes=[pltpu.VMEM((tm, d), BF16)],
        compiler_params=_cparams(("arbitrary", "arbitrary"), FFN_VMEM_LIMIT),
        name="ffn",
    )(xt, norm_g, modr, modr, modr, w_gu, w_gu, w_dn, final_g, *cast_next)
    return outs[0], tuple(outs[1:])


def _rope_tables(seq, pad_rows):
    rows = seq // GRID_W
    inv_freq = ROPE_THETA ** (-jnp.arange(0, ROPE_AXIS, 2, dtype=F32) / ROPE_AXIS)
    row = jnp.repeat(jnp.arange(rows, dtype=F32), GRID_W)
    col = jnp.tile(jnp.arange(GRID_W, dtype=F32), rows)
    ar = row[:, None] * inv_freq
    ac = col[:, None] * inv_freq
    cos = jnp.concatenate([jnp.cos(ar), jnp.cos(ar), jnp.cos(ac), jnp.cos(ac)], axis=1)
    sin = jnp.concatenate([-jnp.sin(ar), jnp.sin(ar), -jnp.sin(ac), jnp.sin(ac)], axis=1)
    cos = jnp.concatenate([cos, jnp.ones((pad_rows, HEAD_DIM), F32)], axis=0)
    sin = jnp.concatenate([sin, jnp.zeros((pad_rows, HEAD_DIM), F32)], axis=0)
    return cos, sin


def kernel(x, c, ctx, c_ctx, w_ada, b_ada, norm_mix_g, norm_ffn_g, w_in, attn_sink, pool_w, pool_scale,
           sgu_norm_g, sgu_w, sgu_b, w_out, w_gate_up, w_down, final_norm_g):
    bsz, seq, d = x.shape
    ctx_len = ctx.shape[1]
    depth = w_in.shape[0]
    n_lat_rows = bsz * seq
    n_rows = n_lat_rows + bsz * ctx_len

    cond = jnp.concatenate([c, c_ctx[None, :], jnp.zeros((MOD_ROWS - bsz - 1, d), F32)], axis=0)
    b_ada3 = b_ada.reshape(depth, 1, MOD_CHUNKS * d)
    modr = _ada_table(cond, w_ada, b_ada3, 0).reshape(MOD_ROWS * MOD_CHUNKS, 1, d)

    cos, sin = _rope_tables(seq, ROW_TILE)
    big_weights = (w_in, w_out, w_gate_up, w_down)
    w_in_b, w_out_b, w_gu_b, w_dn_b = (w[0].astype(BF16) for w in big_weights)
    pool_w_b = pool_w.astype(BF16)
    sgu_w_b = sgu_w.astype(BF16)
    norm_mix = norm_mix_g.reshape(depth, 1, d)
    norm_ffn = norm_ffn_g.reshape(depth, 1, d)
    final_g = final_norm_g.reshape(1, d)

    xs = (x.reshape(n_lat_rows, d), ctx.reshape(bsz * ctx_len, d))
    for l in range(depth):
        last = l == depth - 1
        rows = n_lat_rows if last else n_rows
        q, kv, puz, *mod_next = _inproj(xs, norm_mix, modr, cos, sin, w_in_b, l,
                                        n_rows=n_rows, n_lat_rows=n_lat_rows, seq=seq,
                                        ada_next=None if last else (cond, w_ada, b_ada3))
        sgu_b_full = jnp.broadcast_to(sgu_b[l][:, :, None], sgu_b.shape[1:] + (HEAD_DIM,))
        mixes = tuple(
            _mixer(q, kv, puz, attn_sink[l], pool_w_b[l], pool_scale[l].reshape(1, -1),
                   sgu_norm_g[l].reshape(1, -1), sgu_w_b[l], sgu_b_full,
                   n_lat_rows=n_lat_rows, seq=seq, ctx_len=ctx_len, windowed=windowed)
            for windowed in ((True,) if last else (True, False)))
        xt = _outproj(mixes, xs, modr, w_out_b, l, seq=seq, n_lat_rows=n_lat_rows)
        xt, next_w = _ffn(xt, norm_ffn, modr, w_gu_b, w_dn_b, final_g, l, rows=rows, seq=seq,
                          n_lat_rows=n_lat_rows, final_norm=last,
                          cast_next=() if last else big_weights)
        if not last:
            w_in_b, w_out_b, w_gu_b, w_dn_b = next_w
            modr = mod_next[0].reshape(MOD_ROWS * MOD_CHUNKS, 1, d)
        xs = (xt,)
    return xt.reshape(bsz, seq, d)
```

```python
import functools

import jax
import jax.numpy as jnp
from jax import lax
from jax.experimental import pallas as pl
from jax.experimental.pallas import tpu as pltpu

F32 = jnp.float32
BF16 = jnp.bfloat16

GRID_W = 64
HEAD_DIM = 128
GROUP = 4
WINDOW_BLOCK = 128
ROPE_THETA = 10000.0
ROPE_AXIS = HEAD_DIM // 2
POOL_WINDOWS = (2, 4, 8, 16)
POOL_HALO = 8
EPS = 1e-6
NEG = -1e30
LOG2E = 1.4426950408889634

MOD_ROWS = 8
MOD_CHUNKS = 6

LANES = 128
BF16_SUBLANES = 16
VMEM_LIMIT = 56 * 1024 * 1024
FFN_VMEM_LIMIT = 62 * 1024 * 1024
ROW_TILE = 512
INPROJ_SUBTILES = 2
FFN_EDGE_SUBTILES = 2
FFN_ROW_TILE = 1024
FFN_COL_TILE = 512
NORM_CHUNK = 32


def _cparams(sem, vmem_limit=VMEM_LIMIT):
    return pltpu.CompilerParams(dimension_semantics=sem, vmem_limit_bytes=vmem_limit)


def _dot(a, b):
    return jnp.dot(a, b, preferred_element_type=F32)


def _mod_index(group, chunk):
    return group * MOD_CHUNKS + chunk


def _ada_chunk(c_ref, w_ref, b_ref):
    cv = c_ref[...]
    a = (cv * jax.nn.sigmoid(cv)).astype(BF16)
    return _dot(a, w_ref[...].astype(BF16)) + b_ref[...]


def _ada_kernel(c_ref, w_ref, b_ref, o_ref):
    o_ref[...] = _ada_chunk(c_ref, w_ref, b_ref)


def _ada_table(cond, w_ada, b_ada3, l):
    _, d, n = w_ada.shape
    tn = 1024
    return pl.pallas_call(
        _ada_kernel,
        grid=(n // tn,),
        in_specs=[
            pl.BlockSpec((MOD_ROWS, d), lambda j: (0, 0)),
            pl.BlockSpec((None, d, tn), lambda j: (l, 0, j)),
            pl.BlockSpec((None, 1, tn), lambda j: (l, 0, j)),
        ],
        out_specs=pl.BlockSpec((MOD_ROWS, tn), lambda j: (0, j)),
        out_shape=jax.ShapeDtypeStruct((MOD_ROWS, n), F32),
        compiler_params=_cparams(("parallel",)),
        name="ada_table",
    )(cond, w_ada, b_ada3)


def _modulated_norm(x, gain, shift):
    ms = jnp.mean(x * x, axis=-1, keepdims=True)
    return x * lax.rsqrt(ms + EPS) * gain + shift


def _pick_rows(lat_ref, ctx_ref, rows, is_lat_tile):
    if ctx_ref is None:
        return lat_ref[rows, :]
    a = lat_ref[rows, :]
    sel = jnp.full(a.shape, is_lat_tile, jnp.int32) > 0
    return jnp.where(sel, a, ctx_ref[rows, :])


def _inproj_kernel(*refs, n_lat_tiles, q_w, kv_w, split_src, next_ada):
    refs = list(refs)
    xa_ref = refs.pop(0)
    xb_ref = refs.pop(0) if split_src else None
    g_ref, sh_ref, sc_ref, cos_ref, sin_ref, w_ref = refs[:6]
    refs = refs[6:]
    if next_ada:
        cond_ref, wada_ref, bada_ref = refs[:3]
        q_ref, kv_ref, puz_ref, modn_ref, hb_ref = refs[3:]
    else:
        q_ref, kv_ref, puz_ref, hb_ref = refs
    i = pl.program_id(0)
    tm = hb_ref.shape[0]
    is_lat = (i < n_lat_tiles).astype(jnp.int32)

    gain = g_ref[...] * (1.0 + sc_ref[...])
    shift = sh_ref[...]

    sub = tm // INPROJ_SUBTILES
    lane = lax.broadcasted_iota(jnp.int32, (sub, HEAD_DIM), 1)
    first_half = (lane % ROPE_AXIS) < (ROPE_AXIS // 2)

    for s in range(INPROJ_SUBTILES):
        for c in range(sub // NORM_CHUNK):
            rows = slice(s * sub + c * NORM_CHUNK, s * sub + (c + 1) * NORM_CHUNK)
            xs = _pick_rows(xa_ref, xb_ref, rows, is_lat)
            hb_ref[rows, :] = _modulated_norm(xs, gain, shift).astype(BF16)
        rs = slice(s * sub, (s + 1) * sub)
        if next_ada and s == 0:
            modn_ref[...] = _ada_chunk(cond_ref, wada_ref, bada_ref)
        hb = hb_ref[rs, :]
        pq = _dot(hb, w_ref[:, :q_w])
        pkv = _dot(hb, w_ref[:, q_w:q_w + 2 * kv_w])
        puz_ref[rs, :] = _dot(hb, w_ref[:, q_w + 2 * kv_w:])

        cos = cos_ref[rs, :]
        sin = sin_ref[rs, :]

        def rope(t):
            rot = jnp.where(first_half,
                            pltpu.roll(t, HEAD_DIM - ROPE_AXIS // 2, 1),
                            pltpu.roll(t, ROPE_AXIS // 2, 1))
            return t * cos + rot * sin

        for hd in range(q_w // HEAD_DIM):
            sl = slice(hd * HEAD_DIM, (hd + 1) * HEAD_DIM)
            q_ref[rs, sl] = rope(pq[:, sl]).astype(BF16)
        for hd in range(kv_w // HEAD_DIM):
            sl = slice(hd * HEAD_DIM, (hd + 1) * HEAD_DIM)
            kv_ref[rs, sl] = rope(pkv[:, sl]).astype(BF16)
        kv_ref[rs, kv_w:] = pkv[:, kv_w:].astype(BF16)


def _row_sources(x_lat, x_ctx, tm, n_lat_tiles):
    d = x_lat.shape[1]
    lat_map = lambda i: (jnp.minimum(i, n_lat_tiles - 1), 0)
    ctx_map = lambda i: (jnp.maximum(i - n_lat_tiles, 0), 0)
    return [pl.BlockSpec((tm, d), lat_map), pl.BlockSpec((tm, d), ctx_map)], [x_lat, x_ctx]


def _inproj(xs, norm_g, modr, cos, sin, w_in, l, *, n_rows, n_lat_rows, seq, ada_next=None):
    d = xs[0].shape[1]
    in_w = w_in.shape[-1]
    q_w = d // 2
    kv_w = q_w // GROUP
    puz_w = in_w - q_w - 2 * kv_w
    tm = ROW_TILE
    tiles_per_seq = seq // tm
    n_lat_tiles = n_lat_rows // tm
    n_groups = n_lat_tiles // tiles_per_seq
    split_src = len(xs) == 2

    def mod_idx(k):
        return lambda i: (_mod_index(jnp.minimum(i // tiles_per_seq, n_groups), k), 0, 0)

    def rope_idx(i):
        return (jnp.where(i < n_lat_tiles, i % tiles_per_seq, tiles_per_seq), 0)

    if split_src:
        x_specs, x_args = _row_sources(xs[0], xs[1], tm, n_lat_tiles)
    else:
        x_specs, x_args = [pl.BlockSpec((tm, d), lambda i: (i, 0))], [xs[0]]

    ada_specs, ada_args, ada_out_specs, ada_out_shapes = [], [], [], []
    if ada_next is not None:
        cond, w_ada, b_ada3 = ada_next
        n = w_ada.shape[-1]
        tn = n // n_lat_tiles
        chunk = lambda i: jnp.minimum(i, n_lat_tiles - 1)
        ada_specs = [pl.BlockSpec((MOD_ROWS, d), lambda i: (0, 0)),
                     pl.BlockSpec((None, d, tn), lambda i: (l + 1, 0, chunk(i))),
                     pl.BlockSpec((None, 1, tn), lambda i: (l + 1, 0, chunk(i)))]
        ada_args = [cond, w_ada, b_ada3]
        ada_out_specs = [pl.BlockSpec((MOD_ROWS, tn), lambda i: (0, chunk(i)))]
        ada_out_shapes = [jax.ShapeDtypeStruct((MOD_ROWS, n), F32)]

    kern = functools.partial(_inproj_kernel, n_lat_tiles=n_lat_tiles, q_w=q_w, kv_w=kv_w,
                             split_src=split_src, next_ada=ada_next is not None)
    return pl.pallas_call(
        kern,
        grid=(n_rows // tm,),
        in_specs=x_specs + [
            pl.BlockSpec((None, 1, d), lambda i: (l, 0, 0)),
            pl.BlockSpec((None, 1, d), mod_idx(0)),
            pl.BlockSpec((None, 1, d), mod_idx(1)),
            pl.BlockSpec((tm, HEAD_DIM), rope_idx),
            pl.BlockSpec((tm, HEAD_DIM), rope_idx),
            pl.BlockSpec((d, in_w), lambda i: (0, 0), pipeline_mode=pl.Buffered(1)),
        ] + ada_specs,
        out_specs=[
            pl.BlockSpec((tm, q_w), lambda i: (i, 0)),
            pl.BlockSpec((tm, 2 * kv_w), lambda i: (i, 0)),
            pl.BlockSpec((tm, puz_w), lambda i: (i, 0)),
        ] + ada_out_specs,
        out_shape=[
            jax.ShapeDtypeStruct((n_rows, q_w), BF16),
            jax.ShapeDtypeStruct((n_rows, 2 * kv_w), BF16),
            jax.ShapeDtypeStruct((n_rows, puz_w), F32),
        ] + ada_out_shapes,
        scratch_shapes=[pltpu.VMEM((tm, d), BF16)],
        compiler_params=_cparams(("arbitrary",), VMEM_LIMIT if ada_next is None else FFN_VMEM_LIMIT),
        name="in_proj",
    )(*x_args, norm_g, modr, modr, cos, sin, w_in, *ada_args)


def _gelu(x):
    return 0.5 * x * (1.0 + lax.erf(x * (0.5 ** 0.5)))


def _mixer_kernel(sink_ref, q_ref, *refs, windowed, blocks_per_seq, n_kv_heads, pool_w, sgu_w):
    if windowed:
        kvp_ref, kvc_ref, kvn_ref = refs[:3]
        refs = refs[3:]
    kvx_ref, puz_ref, php_ref, phn_ref, pw_ref, ps_ref, sg_ref, sw_ref, sb_ref, o_ref, ext_ref = refs
    blk = WINDOW_BLOCK
    hd = HEAD_DIM
    n = pl.program_id(0) % blocks_per_seq
    nblk = blocks_per_seq
    has_prev = (n > 0).astype(jnp.int32)
    has_next = (n < nblk - 1).astype(jnp.int32)
    kv_w = n_kv_heads * hd
    q_w = kv_w * GROUP
    logit_scale = hd ** -0.5 * LOG2E

    def attend(keys, vals, tile_masks, kvh):
        nk = keys.shape[0]
        qs = jnp.concatenate(
            [q_ref[:, (kvh * GROUP + g) * hd:(kvh * GROUP + g + 1) * hd] for g in range(GROUP)],
            axis=0)
        s = lax.dot_general(qs, keys, (((1,), (1,)), ((), ())),
                            preferred_element_type=F32) * logit_scale
        vals1 = jnp.concatenate([vals, jnp.ones((nk, hd), BF16)], axis=1)
        es, sink_terms = [], []
        for g in range(GROUP):
            tiles = []
            for t, mask in enumerate(tile_masks):
                st = s[g * blk:(g + 1) * blk, t * blk:(t + 1) * blk]
                tiles.append(st if mask is None else jnp.where(mask, st, NEG))
            mt = tiles[0]
            for st in tiles[1:]:
                mt = jnp.maximum(mt, st)
            snk = sink_ref[kvh * GROUP + g] * LOG2E
            m = jnp.maximum(jnp.max(mt, axis=-1, keepdims=True), snk)
            es.append(jnp.concatenate([jnp.exp2(st - m).astype(BF16) for st in tiles], axis=1))
            sink_terms.append(jnp.exp2(snk - m))
        o = _dot(jnp.concatenate(es, axis=0), vals1)
        for g in range(GROUP):
            h = kvh * GROUP + g
            og = o[g * blk:(g + 1) * blk]
            o_ref[:, h * hd:(h + 1) * hd] = (og[:, :hd] / (og[:, hd:] + sink_terms[g])).astype(BF16)

    if windowed:
        qi = lax.broadcasted_iota(jnp.int32, (blk, blk), 0)
        kj = lax.broadcasted_iota(jnp.int32, (blk, blk), 1)
        mask_prev = kj >= qi + (1 - has_prev) * blk
        mask_next = kj <= qi - (1 - has_next) * blk
        masks = [mask_prev, None, mask_next] + [None] * (kvx_ref.shape[0] // blk)
        for kvh in range(n_kv_heads):
            ks = slice(kvh * hd, (kvh + 1) * hd)
            vs = slice(kv_w + kvh * hd, kv_w + (kvh + 1) * hd)
            keys = jnp.concatenate([kvp_ref[:, ks], kvc_ref[:, ks], kvn_ref[:, ks], kvx_ref[:, ks]], axis=0)
            vals = jnp.concatenate([kvp_ref[:, vs], kvc_ref[:, vs], kvn_ref[:, vs], kvx_ref[:, vs]], axis=0)
            attend(keys, vals, masks, kvh)
    else:
        masks = [None] * (kvx_ref.shape[0] // blk)
        for kvh in range(n_kv_heads):
            attend(kvx_ref[:, kvh * hd:(kvh + 1) * hd],
                   kvx_ref[:, kv_w + kvh * hd:kv_w + (kvh + 1) * hd], masks, kvh)

    halo = POOL_HALO
    keep_prev = jnp.full((halo, pool_w), has_prev, jnp.int32) > 0
    keep_next = jnp.full((halo, pool_w), has_next, jnp.int32) > 0
    ext_ref[0:halo, :] = jnp.where(keep_prev, php_ref[...], 0.0)
    ext_ref[halo + blk:, :] = jnp.where(keep_next, phn_ref[...], 0.0)
    ext_ref[halo:halo + blk, :] = puz_ref[:, :pool_w]
    pos = n * blk + lax.broadcasted_iota(jnp.int32, (blk, hd), 0)
    seq_len = nblk * blk
    for g, win in enumerate(POOL_WINDOWS):
        cs = slice(g * hd, (g + 1) * hd)
        half = win // 2
        acc = ext_ref[halo - half:halo - half + blk, cs]
        for d in range(-half + 1, half):
            acc = acc + ext_ref[halo + d:halo + d + blk, cs]
        cnt = (jnp.minimum(pos + half, seq_len) - jnp.maximum(pos - half, 0)).astype(F32)
        pooled = acc / cnt - puz_ref[:, cs]
        y = _dot(pooled.astype(BF16), pw_ref[g]) * ps_ref[:, cs]
        o_ref[:, q_w + g * hd:q_w + (g + 1) * hd] = y.astype(BF16)

    for g in range(sgu_w // hd):
        cs = slice(g * hd, (g + 1) * hd)
        u = _gelu(puz_ref[:, pool_w + g * hd:pool_w + (g + 1) * hd])
        z = _gelu(puz_ref[:, pool_w + sgu_w + g * hd:pool_w + sgu_w + (g + 1) * hd])
        mu = jnp.mean(z, axis=-1, keepdims=True)
        zc = z - mu
        var = jnp.mean(zc * zc, axis=-1, keepdims=True)
        zn = zc * lax.rsqrt(var + EPS) * sg_ref[:, cs]
        mixed = _dot(sw_ref[g], zn.astype(BF16)) + sb_ref[g]
        o_ref[:, q_w + pool_w + g * hd:q_w + pool_w + (g + 1) * hd] = (u * mixed).astype(BF16)


def _mixer(q, kv, puz, sink, pool_w_l, pool_scale_l, sgu_g_l, sgu_w_l, sgu_b_l, *,
           n_lat_rows, seq, ctx_len, windowed):
    blk = WINDOW_BLOCK
    q_w = q.shape[1]
    kv_w = kv.shape[1] // 2
    pool_w = pool_scale_l.shape[-1]
    sgu_w = sgu_g_l.shape[-1]
    d_mix = q_w + pool_w + sgu_w
    per_seq = (seq if windowed else ctx_len) // blk
    base = 0 if windowed else n_lat_rows // blk
    rows = n_lat_rows if windowed else q.shape[0] - n_lat_rows
    first_ctx_block = n_lat_rows // ctx_len
    hpb = blk // POOL_HALO

    def cur(i):
        return (base + i, 0)

    def prev_idx(i):
        return (base + jnp.where(i % per_seq > 0, i - 1, i), 0)

    def next_idx(i):
        return (base + jnp.where(i % per_seq < per_seq - 1, i + 1, i), 0)

    def ctx_idx(i):
        return (first_ctx_block + i // per_seq, 0)

    def halo_prev_idx(i):
        return (jnp.where(i % per_seq > 0, (base + i) * hpb - 1, (base + i) * hpb), 0)

    def halo_next_idx(i):
        return (jnp.where(i % per_seq < per_seq - 1, (base + i + 1) * hpb, (base + i) * hpb), 0)

    kern = functools.partial(
        _mixer_kernel, windowed=windowed, blocks_per_seq=per_seq,
        n_kv_heads=kv_w // HEAD_DIM, pool_w=pool_w, sgu_w=sgu_w)
    n_pool = pool_w // HEAD_DIM
    n_sgu = sgu_w // HEAD_DIM
    window_specs, window_args = [], []
    if windowed:
        window_specs = [pl.BlockSpec((blk, 2 * kv_w), prev_idx),
                        pl.BlockSpec((blk, 2 * kv_w), cur),
                        pl.BlockSpec((blk, 2 * kv_w), next_idx)]
        window_args = [kv, kv, kv]
    return pl.pallas_call(
        kern,
        grid=(rows // blk,),
        in_specs=[
            pl.BlockSpec(memory_space=pltpu.SMEM),
            pl.BlockSpec((blk, q_w), cur),
        ] + window_specs + [
            pl.BlockSpec((ctx_len, 2 * kv_w), ctx_idx),
            pl.BlockSpec((blk, pool_w + 2 * sgu_w), cur),
            pl.BlockSpec((POOL_HALO, pool_w), halo_prev_idx),
            pl.BlockSpec((POOL_HALO, pool_w), halo_next_idx),
            pl.BlockSpec((n_pool, HEAD_DIM, HEAD_DIM), lambda i: (0, 0, 0)),
            pl.BlockSpec((1, pool_w), lambda i: (0, 0)),
            pl.BlockSpec((1, sgu_w), lambda i: (0, 0)),
            pl.BlockSpec((n_sgu, blk, blk), lambda i: (0, 0, 0)),
            pl.BlockSpec((n_sgu, blk, HEAD_DIM), lambda i: (0, 0, 0)),
        ],
        out_specs=pl.BlockSpec((blk, d_mix), lambda i: (i, 0)),
        out_shape=jax.ShapeDtypeStruct((rows, d_mix), BF16),
        scratch_shapes=[pltpu.VMEM((blk + 2 * POOL_HALO, pool_w), F32)],
        compiler_params=_cparams(("parallel",)),
        name="mixer_lat" if windowed else "mixer_ctx",
    )(sink, q, *window_args, kv, puz, puz, puz, pool_w_l, pool_scale_l, sgu_g_l, sgu_w_l, sgu_b_l)


def _outproj_kernel(*refs, n_lat_tiles, split_mix, split_x):
    refs = list(refs)
    ma_ref = refs.pop(0)
    mb_ref = refs.pop(0) if split_mix else None
    xa_ref = refs.pop(0)
    xb_ref = refs.pop(0) if split_x else None
    gate_ref, w_ref, o_ref = refs
    is_lat = (pl.program_id(0) < n_lat_tiles).astype(jnp.int32)
    x = _pick_rows(xa_ref, xb_ref, slice(None), is_lat)
    mix = _pick_rows(ma_ref, mb_ref, slice(None), is_lat)
    o_ref[...] = x + gate_ref[...] * _dot(mix, w_ref[...])


def _outproj(mixes, xs, modr, w_out, l, *, seq, n_lat_rows):
    rows = sum(m.shape[0] for m in mixes)
    d_mix = mixes[0].shape[1]
    d = xs[0].shape[1]
    tm = ROW_TILE
    tiles_per_seq = seq // tm
    n_lat_tiles = n_lat_rows // tm
    n_groups = n_lat_rows // seq

    def sources(arrs):
        if len(arrs) == 2:
            return _row_sources(arrs[0], arrs[1], tm, n_lat_tiles)
        return [pl.BlockSpec((tm, arrs[0].shape[1]), lambda i: (i, 0))], [arrs[0]]

    mix_specs, mix_args = sources(mixes)
    x_specs, x_args = sources(xs)
    kern = functools.partial(_outproj_kernel, n_lat_tiles=n_lat_tiles,
                             split_mix=len(mixes) == 2, split_x=len(xs) == 2)
    return pl.pallas_call(
        kern,
        grid=(rows // tm,),
        in_specs=mix_specs + x_specs + [
            pl.BlockSpec((None, 1, d),
                         lambda i: (_mod_index(jnp.minimum(i // tiles_per_seq, n_groups), 2), 0, 0)),
            pl.BlockSpec((d_mix, d), lambda i: (0, 0), pipeline_mode=pl.Buffered(1)),
        ],
        out_specs=pl.BlockSpec((tm, d), lambda i: (i, 0)),
        out_shape=jax.ShapeDtypeStruct((rows, d), F32),
        compiler_params=_cparams(("parallel",)),
        name="out_proj",
    )(*mix_args, *x_args, modr, w_out)


def _ffn_kernel(*refs, final_norm, n_cast):
    (x_ref, g_ref, sh_ref, sc_ref, gate_ref, wg_ref, wu_ref, wd_ref, fg_ref), refs = refs[:9], refs[9:]
    cast_src, refs = refs[:n_cast], refs[n_cast:]
    o_ref, refs = refs[0], refs[1:]
    cast_dst, (xn_ref,) = refs[:n_cast], refs[n_cast:]
    f = pl.program_id(1)
    nf = pl.num_programs(1)
    tm, d = x_ref.shape

    casts = list(zip(cast_src, cast_dst))
    n_down = d // FFN_COL_TILE

    def swiglu_rows(rs, mode, with_casts):
        xn = xn_ref[rs, :]
        gg = _dot(xn, wg_ref[...])
        uu = _dot(xn, wu_ref[...])
        act = (gg * jax.nn.sigmoid(gg) * uu).astype(BF16)
        for cidx in range(n_down):
            if with_casts:
                for src, dst in casts[cidx::n_down]:
                    dst[...] = src[...].astype(BF16)
            cs = slice(cidx * FFN_COL_TILE, (cidx + 1) * FFN_COL_TILE)
            down = _dot(act, wd_ref[:, cs])
            if mode == "first":
                o_ref[rs, cs] = down
            elif mode == "add":
                o_ref[rs, cs] += down
            else:
                o_ref[rs, cs] = x_ref[rs, cs] + gate_ref[:, cs] * (o_ref[rs, cs] + down)

    sub = tm // FFN_EDGE_SUBTILES

    @pl.when(f == 0)
    def _():
        gain = g_ref[...] * (1.0 + sc_ref[...])
        shift = sh_ref[...]
        for s in range(FFN_EDGE_SUBTILES):
            for c in range(sub // NORM_CHUNK):
                rows = slice(s * sub + c * NORM_CHUNK, s * sub + (c + 1) * NORM_CHUNK)
                xn_ref[rows, :] = _modulated_norm(x_ref[rows, :], gain, shift).astype(BF16)
            swiglu_rows(slice(s * sub, (s + 1) * sub), "first", with_casts=s == 0)

    @pl.when(jnp.logical_and(f > 0, f < nf - 1))
    def _():
        swiglu_rows(slice(None), "add", with_casts=True)

    @pl.when(f == nf - 1)
    def _():
        for s in range(FFN_EDGE_SUBTILES):
            swiglu_rows(slice(s * sub, (s + 1) * sub), "last", with_casts=s == 0)
            if final_norm:
                for c in range(sub // NORM_CHUNK):
                    rows = slice(s * sub + c * NORM_CHUNK, s * sub + (c + 1) * NORM_CHUNK)
                    y = o_ref[rows, :]
                    ms = jnp.mean(y * y, axis=-1, keepdims=True)
                    o_ref[rows, :] = y * lax.rsqrt(ms + EPS) * fg_ref[...]


def _cast_chunks(weights, l_next, n_i, n_f):
    def split(n, unit, limit):
        return max(k for k in range(1, limit + 1) if n % (k * unit) == 0)

    in_specs, out_specs, out_shapes = [], [], []
    for w in weights:
        _, r, c = w.shape
        rows_i = (split(r, BF16_SUBLANES, n_i), split(c, LANES, n_f))
        rows_f = (split(r, BF16_SUBLANES, n_f), split(c, LANES, n_i))
        if rows_i[0] * rows_i[1] >= rows_f[0] * rows_f[1]:
            kr, kc = rows_i
            idx = lambda i, f, kr=kr, kc=kc: (
                jnp.minimum(i, kr - 1), jnp.where(i < kr, jnp.minimum(f, kc - 1), kc - 1))
        else:
            kr, kc = rows_f
            idx = lambda i, f, kr=kr, kc=kc: (
                jnp.where(i < kc, jnp.minimum(f, kr - 1), kr - 1), jnp.minimum(i, kc - 1))
        blk = (r // kr, c // kc)
        in_specs.append(pl.BlockSpec((None,) + blk, lambda i, f, idx=idx: (l_next,) + idx(i, f)))
        out_specs.append(pl.BlockSpec(blk, idx))
        out_shapes.append(jax.ShapeDtypeStruct((r, c), BF16))
    return in_specs, out_specs, out_shapes


def _ffn(xt, norm_g, modr, w_gu, w_dn, final_g, l, *, rows, seq, n_lat_rows, final_norm, cast_next=()):
    d = xt.shape[1]
    d_ff = w_dn.shape[0]
    tm = FFN_ROW_TILE
    tf = FFN_COL_TILE
    nf = d_ff // tf
    tiles_per_seq = seq // tm
    n_groups = n_lat_rows // seq

    def mod_idx(k):
        return lambda i, f: (_mod_index(jnp.minimum(i // tiles_per_seq, n_groups), k), 0, 0)

    cast_in, cast_out, cast_shapes = _cast_chunks(cast_next, l + 1, rows // tm, nf)
    kern = functools.partial(_ffn_kernel, final_norm=final_norm, n_cast=len(cast_next))
    outs = pl.pallas_call(
        kern,
        grid=(rows // tm, nf),
        in_specs=[
            pl.BlockSpec((tm, d), lambda i, f: (i, 0)),
            pl.BlockSpec((None, 1, d), lambda i, f: (l, 0, 0)),
            pl.BlockSpec((None, 1, d), mod_idx(3)),
            pl.BlockSpec((None, 1, d), mod_idx(4)),
            pl.BlockSpec((None, 1, d), mod_idx(5)),
            pl.BlockSpec((d, tf), lambda i, f: (0, f)),
            pl.BlockSpec((d, tf), lambda i, f: (0, nf + f)),
            pl.BlockSpec((tf, d), lambda i, f: (f, 0)),
            pl.BlockSpec((1, d), lambda i, f: (0, 0)),
        ] + cast_in,
        out_specs=[pl.BlockSpec((tm, d), lambda i, f: (i, 0))] + cast_out,
        out_shape=[jax.ShapeDtypeStruct((rows, d), F32)] + cast_shapes,
        scratch_shapes=[pltpu.VMEM((tm, d), BF16)],
        compiler_params=_cparams(("arbitrary", "arbitrary"), FFN_VMEM_LIMIT),
        name="ffn",
    )(xt, norm_g, modr, modr, modr, w_gu, w_gu, w_dn, final_g, *cast_next)
    return outs[0], tuple(outs[1:])


def _rope_tables(seq, pad_rows):
    rows = seq // GRID_W
    inv_freq = ROPE_THETA ** (-jnp.arange(0, ROPE_AXIS, 2, dtype=F32) / ROPE_AXIS)
    row = jnp.repeat(jnp.arange(rows, dtype=F32), GRID_W)
    col = jnp.tile(jnp.arange(GRID_W, dtype=F32), rows)
    ar = row[:, None] * inv_freq
    ac = col[:, None] * inv_freq
    cos = jnp.concatenate([jnp.cos(ar), jnp.cos(ar), jnp.cos(ac), jnp.cos(ac)], axis=1)
    sin = jnp.concatenate([-jnp.sin(ar), jnp.sin(ar), -jnp.sin(ac), jnp.sin(ac)], axis=1)
    cos = jnp.concatenate([cos, jnp.ones((pad_rows, HEAD_DIM), F32)], axis=0)
    sin = jnp.concatenate([sin, jnp.zeros((pad_rows, HEAD_DIM), F32)], axis=0)
    return cos, sin


def kernel(x, c, ctx, c_ctx, w_ada, b_ada, norm_mix_g, norm_ffn_g, w_in, attn_sink, pool_w, pool_scale,
           sgu_norm_g, sgu_w, sgu_b, w_out, w_gate_up, w_down, final_norm_g):
    bsz, seq, d = x.shape
    ctx_len = ctx.shape[1]
    depth = w_in.shape[0]
    n_lat_rows = bsz * seq
    n_rows = n_lat_rows + bsz * ctx_len

    cond = jnp.concatenate([c, c_ctx[None, :], jnp.zeros((MOD_ROWS - bsz - 1, d), F32)], axis=0)
    b_ada3 = b_ada.reshape(depth, 1, MOD_CHUNKS * d)
    modr = _ada_table(cond, w_ada, b_ada3, 0).reshape(MOD_ROWS * MOD_CHUNKS, 1, d)

    cos, sin = _rope_tables(seq, ROW_TILE)
    big_weights = (w_in, w_out, w_gate_up, w_down)
    w_in_b, w_out_b, w_gu_b, w_dn_b = (w[0].astype(BF16) for w in big_weights)
    pool_w_b = pool_w.astype(BF16)
    sgu_w_b = sgu_w.astype(BF16)
    norm_mix = norm_mix_g.reshape(depth, 1, d)
    norm_ffn = norm_ffn_g.reshape(depth, 1, d)
    final_g = final_norm_g.reshape(1, d)

    xs = (x.reshape(n_lat_rows, d), ctx.reshape(bsz * ctx_len, d))
    for l in range(depth):
        last = l == depth - 1
        rows = n_lat_rows if last else n_rows
        q, kv, puz, *mod_next = _inproj(xs, norm_mix, modr, cos, sin, w_in_b, l,
                                        n_rows=n_rows, n_lat_rows=n_lat_rows, seq=seq,
                                        ada_next=None if last else (cond, w_ada, b_ada3))
        sgu_b_full = jnp.broadcast_to(sgu_b[l][:, :, None], sgu_b.shape[1:] + (HEAD_DIM,))
        mixes = tuple(
            _mixer(q, kv, puz, attn_sink[l], pool_w_b[l], pool_scale[l].reshape(1, -1),
                   sgu_norm_g[l].reshape(1, -1), sgu_w_b[l], sgu_b_full,
                   n_lat_rows=n_lat_rows, seq=seq, ctx_len=ctx_len, windowed=windowed)
            for windowed in ((True,) if last else (True, False)))
        xt = _outproj(mixes, xs, modr, w_out_b, l, seq=seq, n_lat_rows=n_lat_rows)
        xt, next_w = _ffn(xt, norm_ffn, modr, w_gu_b, w_dn_b, final_g, l, rows=rows, seq=seq,
                          n_lat_rows=n_lat_rows, final_norm=last,
                          cast_next=() if last else big_weights)
        if not last:
            w_in_b, w_out_b, w_gu_b, w_dn_b = next_w
            modr = mod_next[0].reshape(MOD_ROWS * MOD_CHUNKS, 1, d)
        xs = (xt,)
    return xt.reshape(bsz, seq, d)
```
